```python
import jax, jax.numpy as jnp
from jax import lax
import numpy as np

D_MODEL = 1024
BATCH = 4
SEQ = 8192
DEPTH = 2

PLE_DIM = 256
D_FF = ((8 * D_MODEL // 3 + 127) // 128) * 128
CONV_CH = D_MODEL
CONV_K = 31
RET_HEADS = 4
RET_QK_DIM = D_MODEL // RET_HEADS
RET_V_DIM = 2 * D_MODEL // RET_HEADS
RET_QK = RET_HEADS * RET_QK_DIM
RET_V = RET_HEADS * RET_V_DIM
CHUNK = 128
ROPE_BASE = 10000.0
EPS = 1e-6
IN_W = 2 * CONV_CH + 2 * RET_QK + 2 * RET_V + 2 * D_MODEL

kernel_name = "hybrid_conformer_conv_retention_gated_macaron"


def rms_norm(t, g):
    tf = t.astype(jnp.float32)
    y = tf * lax.rsqrt(jnp.mean(tf * tf, axis=-1, keepdims=True) + EPS)
    return (y * g.astype(jnp.float32)).astype(t.dtype)


def layer_norm(t, g, b):
    tf = t.astype(jnp.float32)
    mu = jnp.mean(tf, axis=-1, keepdims=True)
    var = jnp.mean(jnp.square(tf - mu), axis=-1, keepdims=True)
    y = (tf - mu) * lax.rsqrt(var + EPS)
    return (y * g.astype(jnp.float32) + b.astype(jnp.float32)).astype(t.dtype)


def swiglu(t, w_gate, w_up, w_down):
    return (jax.nn.silu(t @ w_gate) * (t @ w_up)) @ w_down


def rotary(t, positions):
    half = t.shape[-1] // 2
    inv = ROPE_BASE ** (-jnp.arange(half, dtype=jnp.float32) / half)
    ang = positions.astype(jnp.float32)[..., None] * inv
    cos = jnp.cos(ang)[:, :, None, :]
    sin = jnp.sin(ang)[:, :, None, :]
    t1 = t[..., :half].astype(jnp.float32)
    t2 = t[..., half:].astype(jnp.float32)
    return jnp.concatenate([t1 * cos - t2 * sin, t2 * cos + t1 * sin], axis=-1).astype(t.dtype)


def conv_module(a, w_dw, b_dw, ln_g, ln_b, w_o):
    glu = a[..., :CONV_CH] * jax.nn.sigmoid(a[..., CONV_CH:])
    y = lax.conv_general_dilated(
        glu, w_dw[:, None, :].astype(glu.dtype), window_strides=(1,),
        padding=[(CONV_K - 1, 0)], dimension_numbers=("NWC", "WIO", "NWC"),
        feature_group_count=CONV_CH) + b_dw
    y = layer_norm(y, ln_g, ln_b)
    return jax.nn.silu(y) @ w_o


def retention(q, k, v):
    B, S, H, dk = q.shape
    dv = v.shape[-1]
    n = S // CHUNK
    log_g = jnp.log1p(-jnp.exp2(-5.0 - jnp.arange(H, dtype=jnp.float32)))
    idx = jnp.arange(CHUNK, dtype=jnp.float32)
    rel = idx[:, None] - idx[None, :]
    intra = jnp.where(rel >= 0, jnp.exp(log_g[:, None, None] * jnp.maximum(rel, 0.0)), 0.0)
    q_dec = jnp.exp(log_g[:, None] * (idx + 1.0))
    k_dec = jnp.exp(log_g[:, None] * (CHUNK - 1.0 - idx))
    chunk_dec = jnp.exp(log_g * CHUNK)

    def to_chunks(t):
        return t.astype(jnp.float32).reshape(B, n, CHUNK, H, t.shape[-1]).transpose(1, 0, 3, 2, 4)

    qc = to_chunks(q)
    kc = to_chunks(k) * (dk ** -0.5)
    vc = to_chunks(v)

    def step(state, qkv):
        qb, kb, vb = qkv
        scores = jnp.einsum("bhid,bhjd->bhij", qb, kb) * intra
        inner = jnp.einsum("bhij,bhje->bhie", scores, vb)
        cross = jnp.einsum("bhid,bhde->bhie", qb * q_dec[..., None], state)
        state = chunk_dec[:, None, None] * state + jnp.einsum(
            "bhjd,bhje->bhde", kb * k_dec[..., None], vb)
        return state, inner + cross

    s0 = jnp.zeros((B, H, dk, dv), jnp.float32)
    _, out = lax.scan(step, s0, (qc, kc, vc))
    return out.transpose(1, 0, 3, 2, 4).reshape(B, S, H, dv)


def setup_inputs(seed: int = 0) -> dict:
    key = jax.random.key(seed)
    ks = iter(jax.random.split(key, 40))

    def w(shape, fan_in):
        return jax.random.normal(next(ks), shape, jnp.float32) * (fan_in ** -0.5)

    def gain(shape):
        return 1.0 + 0.02 * jax.random.normal(next(ks), shape, jnp.float32)

    def small(shape):
        return 0.02 * jax.random.normal(next(ks), shape, jnp.float32)

    x = jax.random.normal(next(ks), (BATCH, SEQ, D_MODEL), jnp.float32)
    p = jax.random.normal(next(ks), (DEPTH, BATCH, SEQ, PLE_DIM), jnp.float32)
    offset = jax.random.randint(next(ks), (BATCH, 1), 0, 1024, dtype=jnp.int32)
    positions = offset + jnp.arange(SEQ, dtype=jnp.int32)[None, :]
    return {
        "x": x, "p": p, "positions": positions,
        "ffn1_norm": gain((DEPTH, D_MODEL)),
        "ffn1_w_gate": w((DEPTH, D_MODEL, D_FF), D_MODEL),
        "ffn1_w_up": w((DEPTH, D_MODEL, D_FF), D_MODEL),
        "ffn1_w_down": w((DEPTH, D_FF, D_MODEL), D_FF),
        "mix_norm": gain((DEPTH, D_MODEL)),
        "w_in": w((DEPTH, D_MODEL, IN_W), D_MODEL),
        "b_gates": small((DEPTH, 2, D_MODEL)),
        "conv_w": w((DEPTH, CONV_K, CONV_CH), CONV_K),
        "conv_b": small((DEPTH, CONV_CH)),
        "conv_ln_g": gain((DEPTH, CONV_CH)),
        "conv_ln_b": small((DEPTH, CONV_CH)),
        "w_conv_o": w((DEPTH, CONV_CH, D_MODEL), CONV_CH),
        "w_ret_o": w((DEPTH, RET_V, D_MODEL), RET_V),
        "w_out": w((DEPTH, D_MODEL, D_MODEL), D_MODEL),
        "ffn2_norm": gain((DEPTH, D_MODEL)),
        "ffn2_w_gate": w((DEPTH, D_MODEL, D_FF), D_MODEL),
        "ffn2_w_up": w((DEPTH, D_MODEL, D_FF), D_MODEL),
        "ffn2_w_down": w((DEPTH, D_FF, D_MODEL), D_FF),
        "ple_norm": gain((DEPTH, D_MODEL)),
        "w_ple_gate": w((DEPTH, D_MODEL, D_MODEL), D_MODEL),
        "w_ple_proj": w((DEPTH, PLE_DIM, D_MODEL), PLE_DIM),
        "ple_proj_norm": gain((DEPTH, D_MODEL)),
        "final_norm": gain((D_MODEL,)),
    }


def reference(x, p, positions, ffn1_norm, ffn1_w_gate, ffn1_w_up, ffn1_w_down,
              mix_norm, w_in, b_gates, conv_w, conv_b, conv_ln_g, conv_ln_b,
              w_conv_o, w_ret_o, w_out, ffn2_norm, ffn2_w_gate, ffn2_w_up, ffn2_w_down,
              ple_norm, w_ple_gate, w_ple_proj, ple_proj_norm, final_norm):
    B, S, _ = x.shape
    split_idx = np.cumsum([2 * CONV_CH, RET_QK, RET_QK, RET_V, RET_V, D_MODEL]).tolist()
    h = x
    for i in range(DEPTH):
        h = h + 0.5 * swiglu(rms_norm(h, ffn1_norm[i]), ffn1_w_gate[i], ffn1_w_up[i], ffn1_w_down[i])

        u = rms_norm(h, mix_norm[i])
        z = u @ w_in[i]
        a_conv, q, k, v, g_ret, z_ga, z_gb = jnp.split(z, split_idx, axis=-1)

        y_a = conv_module(a_conv, conv_w[i], conv_b[i], conv_ln_g[i], conv_ln_b[i], w_conv_o[i])

        qh = rotary(q.reshape(B, S, RET_HEADS, RET_QK_DIM), positions)
        kh = rotary(k.reshape(B, S, RET_HEADS, RET_QK_DIM), positions)
        o = retention(qh, kh, v.reshape(B, S, RET_HEADS, RET_V_DIM))
        o = o * lax.rsqrt(jnp.mean(o * o, axis=-1, keepdims=True) + EPS)
        o = (jax.nn.silu(g_ret.astype(jnp.float32)) * o.reshape(B, S, RET_V)).astype(h.dtype)
        y_b = o @ w_ret_o[i]

        m = jax.nn.sigmoid(z_ga + b_gates[i, 0]) * y_a + jax.nn.sigmoid(z_gb + b_gates[i, 1]) * y_b
        h = h + m @ w_out[i]

        h = h + 0.5 * swiglu(rms_norm(h, ffn2_norm[i]), ffn2_w_gate[i], ffn2_w_up[i], ffn2_w_down[i])

        e = rms_norm(p[i].astype(h.dtype) @ w_ple_proj[i], ple_proj_norm[i])
        gate = jax.nn.sigmoid(rms_norm(h, ple_norm[i]) @ w_ple_gate[i])
        h = h + gate * e
    return rms_norm(h, final_norm)
```

```python
import functools
import math

import jax
import jax.numpy as jnp
import numpy as np
from jax import lax
from jax.experimental import pallas as pl
from jax.experimental.pallas import tpu as pltpu

F32 = jnp.float32
BF16 = jnp.bfloat16

EPS = 1e-6
ROPE_BASE = 10000.0
RET_HEADS = 4
RET_CHUNK = 128
CONV_HALO = 32
CONV_ROWS = 64
LANES = 128

VMEM_LIMIT_BYTES = 56 * 1024 * 1024


def _params(n_axes):
    return pltpu.CompilerParams(
        dimension_semantics=("arbitrary",) * n_axes,
        vmem_limit_bytes=VMEM_LIMIT_BYTES,
    )


def _rms(x, g):
    return x * lax.rsqrt(jnp.mean(x * x, axis=-1, keepdims=True) + EPS) * g


def _dot(a, b):
    return jnp.dot(a, b, preferred_element_type=F32)


def _full(shape):
    nd = len(shape)
    return pl.BlockSpec(shape, lambda *_: (0,) * nd)


def _ffn_body(h_ref, g_ref, wg_ref, wu_ref, wd_ref, o_ref, *, chunks):
    x = h_ref[...]
    n = _rms(x, g_ref[...]).astype(BF16)
    acc = jnp.zeros(x.shape, F32)
    for c0, cw in chunks:
        gate = _dot(n, wg_ref[:, c0:c0 + cw])
        up = _dot(n, wu_ref[:, c0:c0 + cw])
        act = (gate * jax.nn.sigmoid(gate) * up).astype(BF16)
        acc = acc + _dot(act, wd_ref[c0:c0 + cw, :])
    o_ref[...] = x + 0.5 * acc


def _ffn(h, g, wg, wu, wd, *, tm):
    n_tok, d = h.shape
    d_ff = wg.shape[1]
    chunks, c0 = [], 0
    while c0 < d_ff:
        cw = min(768, d_ff - c0)
        chunks.append((c0, cw))
        c0 += cw
    return pl.pallas_call(
        functools.partial(_ffn_body, chunks=tuple(chunks)),
        grid=(n_tok // tm,),
        in_specs=[
            pl.BlockSpec((tm, d), lambda i: (i, 0)),
            _full((1, d)), _full(wg.shape), _full(wu.shape), _full(wd.shape),
        ],
        out_specs=pl.BlockSpec((tm, d), lambda i: (i, 0)),
        out_shape=jax.ShapeDtypeStruct((n_tok, d), F32),
        compiler_params=_params(1),
        name="ffn",
    )(h, g, wg, wu, wd)


def _conv_body(h_ref, g_ref, wa_ref, cw_ref, cb_ref, lng_ref, lnb_ref, wo_ref, o_ref,
               ext_ref, y_ref, *, tm, conv_k):
    ch = o_ref.shape[-1]
    s = pl.program_id(1)

    @pl.when(s == 0)
    def _():
        ext_ref[0:CONV_HALO, :] = jnp.zeros((CONV_HALO, ch), F32)

    @pl.when(s > 0)
    def _():
        ext_ref[0:CONV_HALO, :] = ext_ref[tm:tm + CONV_HALO, :]

    u = _rms(h_ref[0], g_ref[...]).astype(BF16)
    a = _dot(u, wa_ref[...])
    ext_ref[CONV_HALO:CONV_HALO + tm, :] = a[:, :ch] * jax.nn.sigmoid(a[:, ch:])

    off = CONV_HALO - (conv_k - 1)
    def lane_block(cb, carry):
        lanes = pl.ds(pl.multiple_of(cb * LANES, LANES), LANES)
        taps = [cw_ref[j:j + 1, lanes] for j in range(conv_k)]
        bias = cb_ref[:, lanes]
        for r in range(tm // CONV_ROWS):
            base = r * CONV_ROWS
            acc = jnp.broadcast_to(bias, (CONV_ROWS, LANES))
            for j in range(conv_k):
                acc = acc + taps[j] * ext_ref[base + off + j:base + off + j + CONV_ROWS, lanes]
            y_ref[base:base + CONV_ROWS, lanes] = acc
        return carry

    lax.fori_loop(0, ch // LANES, lane_block, 0)

    y = y_ref[...]
    mu = jnp.mean(y, axis=-1, keepdims=True)
    yc = y - mu
    var = jnp.mean(yc * yc, axis=-1, keepdims=True)
    yn = yc * lax.rsqrt(var + EPS) * lng_ref[...] + lnb_ref[...]
    act = (yn * jax.nn.sigmoid(yn)).astype(BF16)
    o_ref[0] = _dot(act, wo_ref[...])


def _conv_mixer(h3, g, wa, conv_w, conv_b, ln_g, ln_b, wo, *, tm):
    b, s, d = h3.shape
    ch = wo.shape[0]
    conv_k = conv_w.shape[0]
    assert conv_k - 1 <= CONV_HALO and tm % CONV_ROWS == 0
    return pl.pallas_call(
        functools.partial(_conv_body, tm=tm, conv_k=conv_k),
        grid=(b, s // tm),
        in_specs=[
            pl.BlockSpec((1, tm, d), lambda i, j: (i, j, 0)),
            _full((1, d)), _full(wa.shape), _full(conv_w.shape), _full((1, ch)),
            _full((1, ch)), _full((1, ch)), _full(wo.shape),
        ],
        out_specs=pl.BlockSpec((1, tm, d), lambda i, j: (i, j, 0)),
        out_shape=jax.ShapeDtypeStruct((b, s, d), F32),
        scratch_shapes=[
            pltpu.VMEM((tm + CONV_HALO, ch), F32),
            pltpu.VMEM((tm, ch), F32),
        ],
        compiler_params=_params(2),
        name="conv_mixer",
    )(h3, g, wa, conv_w, conv_b, ln_g, ln_b, wo)


def _rotate(t, cos, sin):
    half = t.shape[-1] // 2
    t1, t2 = t[:, :half], t[:, half:]
    return jnp.concatenate([t1 * cos - t2 * sin, t2 * cos + t1 * sin], axis=-1)


def _ret_body(h_ref, g_ref, pos_ref, inv_ref, w_ref, wo_ref, o_ref, state_ref, gated_ref,
              *, tm, dk, dv):
    heads = RET_HEADS
    c = RET_CHUNK
    s = pl.program_id(1)

    @pl.when(s == 0)
    def _():
        state_ref[...] = jnp.zeros(state_ref.shape, F32)

    u = _rms(h_ref[0], g_ref[...]).astype(BF16)
    qk = heads * dk
    vw = heads * dv
    q = _dot(u, w_ref[:, 0:qk])
    k = _dot(u, w_ref[:, qk:2 * qk])
    v = _dot(u, w_ref[:, 2 * qk:2 * qk + vw])
    gr = _dot(u, w_ref[:, 2 * qk + vw:2 * qk + 2 * vw])

    ang = pos_ref[0].astype(F32) * inv_ref[...]
    cos, sin = jnp.cos(ang), jnp.sin(ang)

    row = lax.broadcasted_iota(jnp.int32, (c, c), 0)
    col = lax.broadcasted_iota(jnp.int32, (c, c), 1)
    rel = (row - col).astype(F32)
    idx_q = lax.broadcasted_iota(jnp.int32, (c, dk), 0).astype(F32)

    for hd in range(heads):
        log_g = math.log1p(-(2.0 ** (-5.0 - hd)))
        intra = jnp.where(rel >= 0, jnp.exp(log_g * jnp.maximum(rel, 0.0)), 0.0)
        q_dec = jnp.exp(log_g * (idx_q + 1.0))
        k_dec = jnp.exp(log_g * (c - 1.0 - idx_q))
        chunk_dec = math.exp(log_g * c)
        qh = _rotate(q[:, hd * dk:(hd + 1) * dk], cos, sin)
        kh = _rotate(k[:, hd * dk:(hd + 1) * dk], cos, sin) * (dk ** -0.5)
        for ci in range(tm // c):
            rows = slice(ci * c, (ci + 1) * c)
            qb, kb = qh[rows], kh[rows]
            vb = v[rows, hd * dv:(hd + 1) * dv].astype(BF16)
            state = state_ref[hd]
            scores = lax.dot_general(qb.astype(BF16), kb.astype(BF16),
                                     (((1,), (1,)), ((), ())),
                                     preferred_element_type=F32) * intra
            inner = _dot(scores.astype(BF16), vb)
            cross = _dot((qb * q_dec).astype(BF16), state.astype(BF16))
            state_ref[hd] = chunk_dec * state + lax.dot_general(
                (kb * k_dec).astype(BF16), vb, (((0,), (0,)), ((), ())),
                preferred_element_type=F32)
            o = inner + cross
            o = o * lax.rsqrt(jnp.mean(o * o, axis=-1, keepdims=True) + EPS)
            gg = gr[rows, hd * dv:(hd + 1) * dv]
            gated_ref[rows, hd * dv:(hd + 1) * dv] = (gg * jax.nn.sigmoid(gg) * o).astype(BF16)

    o_ref[0] = _dot(gated_ref[...], wo_ref[...])


def _ret_mixer(h3, g, pos3, inv, w, wo, *, tm):
    b, s, d = h3.shape
    dk = d // RET_HEADS
    dv = wo.shape[0] // RET_HEADS
    assert tm % RET_CHUNK == 0 and w.shape[1] == 2 * RET_HEADS * (dk + dv)
    return pl.pallas_call(
        functools.partial(_ret_body, tm=tm, dk=dk, dv=dv),
        grid=(b, s // tm),
        in_specs=[
            pl.BlockSpec((1, tm, d), lambda i, j: (i, j, 0)),
            _full((1, d)),
            pl.BlockSpec((1, tm, 1), lambda i, j: (i, j, 0)),
            _full(inv.shape), _full(w.shape), _full(wo.shape),
        ],
        out_specs=pl.BlockSpec((1, tm, d), lambda i, j: (i, j, 0)),
        out_shape=jax.ShapeDtypeStruct((b, s, d), F32),
        scratch_shapes=[
            pltpu.VMEM((RET_HEADS, dk, dv), F32),
            pltpu.VMEM((tm, RET_HEADS * dv), BF16),
        ],
        compiler_params=_params(2),
        name="ret_mixer",
    )(h3, g, pos3, inv, w, wo)


def _merge_body(h_ref, ya_ref, yb_ref, g_ref, wg_ref, bg_ref, wo_ref, o_ref):
    x = h_ref[...]
    d = x.shape[-1]
    u = _rms(x, g_ref[...]).astype(BF16)
    zg = _dot(u, wg_ref[...])
    m = (jax.nn.sigmoid(zg[:, :d] + bg_ref[0:1, :]) * ya_ref[...]
         + jax.nn.sigmoid(zg[:, d:] + bg_ref[1:2, :]) * yb_ref[...])
    o_ref[...] = x + _dot(m.astype(BF16), wo_ref[...])


def _merge(h, ya, yb, g, wg, bg, wo, *, tm):
    n_tok, d = h.shape
    tile = pl.BlockSpec((tm, d), lambda i: (i, 0))
    return pl.pallas_call(
        _merge_body,
        grid=(n_tok // tm,),
        in_specs=[tile, tile, tile, _full((1, d)), _full(wg.shape), _full(bg.shape),
                  _full(wo.shape)],
        out_specs=tile,
        out_shape=jax.ShapeDtypeStruct((n_tok, d), F32),
        compiler_params=_params(1),
        name="merge",
    )(h, ya, yb, g, wg, bg, wo)


def _ple_body(h_ref, p_ref, gh_ref, wpg_ref, wpe_ref, gp_ref, gf_ref, o_ref, *, final):
    x = h_ref[...]
    e = _rms(_dot(p_ref[...].astype(BF16), wpe_ref[...]), gp_ref[...])
    gate = jax.nn.sigmoid(_dot(_rms(x, gh_ref[...]).astype(BF16), wpg_ref[...]))
    out = x + gate * e
    if final:
        out = _rms(out, gf_ref[...])
    o_ref[...] = out


def _ple(h, p, gh, wpg, wpe, gp, gf, *, tm, final):
    n_tok, d = h.shape
    pd = p.shape[1]
    tile = pl.BlockSpec((tm, d), lambda i: (i, 0))
    return pl.pallas_call(
        functools.partial(_ple_body, final=final),
        grid=(n_tok // tm,),
        in_specs=[tile, pl.BlockSpec((tm, pd), lambda i: (i, 0)), _full((1, d)),
                  _full(wpg.shape), _full(wpe.shape), _full((1, d)), _full((1, d))],
        out_specs=tile,
        out_shape=jax.ShapeDtypeStruct((n_tok, d), F32),
        compiler_params=_params(1),
        name="ple",
    )(h, p, gh, wpg, wpe, gp, gf)


def kernel(x, p, positions, ffn1_norm, ffn1_w_gate, ffn1_w_up, ffn1_w_down, mix_norm, w_in,
           b_gates, conv_w, conv_b, conv_ln_g, conv_ln_b, w_conv_o, w_ret_o, w_out, ffn2_norm,
           ffn2_w_gate, ffn2_w_up, ffn2_w_down, ple_norm, w_ple_gate, w_ple_proj,
           ple_proj_norm, final_norm):
    b, s, d = x.shape
    depth = p.shape[0]
    n_tok = b * s
    ch = w_conv_o.shape[1]
    ret_v = w_ret_o.shape[1]
    conv_in = 2 * ch
    gates_at = w_in.shape[2] - 2 * d

    half = d // RET_HEADS // 2
    inv = (ROPE_BASE ** (-jnp.arange(half, dtype=F32) / half)).reshape(1, half)
    pos3 = positions.reshape(b, s, 1)

    def row(v):
        return v.reshape(1, -1)

    h = x.reshape(n_tok, d)
    for i in range(depth):
        w_in_b = w_in[i].astype(BF16)
        h = _ffn(h, row(ffn1_norm[i]), ffn1_w_gate[i].astype(BF16), ffn1_w_up[i].astype(BF16),
                 ffn1_w_down[i].astype(BF16), tm=512)
        h3 = h.reshape(b, s, d)
        ya = _conv_mixer(h3, row(mix_norm[i]), w_in_b[:, :conv_in], conv_w[i], row(conv_b[i]),
                         row(conv_ln_g[i]), row(conv_ln_b[i]), w_conv_o[i].astype(BF16), tm=512)
        yb = _ret_mixer(h3, row(mix_norm[i]), pos3, inv, w_in_b[:, conv_in:gates_at],
                        w_ret_o[i].astype(BF16), tm=512)
        h = _merge(h, ya.reshape(n_tok, d), yb.reshape(n_tok, d), row(mix_norm[i]),
                   w_in_b[:, gates_at:], b_gates[i], w_out[i].astype(BF16), tm=512)
        h = _ffn(h, row(ffn2_norm[i]), ffn2_w_gate[i].astype(BF16), ffn2_w_up[i].astype(BF16),
                 ffn2_w_down[i].astype(BF16), tm=512)
        h = _ple(h, p[i].reshape(n_tok, -1), row(ple_norm[i]), w_ple_gate[i].astype(BF16),
                 w_ple_proj[i].astype(BF16), row(ple_proj_norm[i]), row(final_norm),
                 tm=512, final=(i == depth - 1))
    return h.reshape(b, s, d)
```

```python
import functools
import math

import jax
import jax.numpy as jnp
import numpy as np
from jax import lax
from jax.experimental import pallas as pl
from jax.experimental.pallas import tpu as pltpu

F32 = jnp.float32
BF16 = jnp.bfloat16

EPS = 1e-6
ROPE_BASE = 10000.0
RET_HEADS = 4
RET_CHUNK = 128
CONV_HALO = 32
CONV_STEPS = 8
LANES = 128
SUBLANES = 8

VMEM_LIMIT_BYTES = 56 * 1024 * 1024


def _params(n_axes):
    return pltpu.CompilerParams(
        dimension_semantics=("arbitrary",) * n_axes,
        vmem_limit_bytes=VMEM_LIMIT_BYTES,
    )


def _rms(x, g):
    return x * lax.rsqrt(jnp.mean(x * x, axis=-1, keepdims=True) + EPS) * g


def _dot(a, b):
    return jnp.dot(a, b, preferred_element_type=F32)


def _full(shape):
    nd = len(shape)
    return pl.BlockSpec(shape, lambda *_: (0,) * nd)


def _cols(w, col0, width):
    assert col0 % width == 0
    return pl.BlockSpec((w.shape[0], width), lambda *_: (0, col0 // width),
                        pipeline_mode=pl.Buffered(1))


def _ffn_body(h_ref, g_ref, wg_ref, wu_ref, wd_ref, o_ref, *, chunks):
    x = h_ref[...]
    n = _rms(x, g_ref[...]).astype(BF16)
    acc = jnp.zeros(x.shape, F32)
    for c0, cw in chunks:
        gate = _dot(n, wg_ref[:, c0:c0 + cw])
        up = _dot(n, wu_ref[:, c0:c0 + cw])
        act = (gate * jax.nn.sigmoid(gate) * up).astype(BF16)
        acc = acc + _dot(act, wd_ref[c0:c0 + cw, :])
    o_ref[...] = x + 0.5 * acc


def _ffn(h, g, wg, wu, wd, *, tm):
    n_tok, d = h.shape
    d_ff = wg.shape[1]
    chunks, c0 = [], 0
    while c0 < d_ff:
        cw = min(768, d_ff - c0)
        chunks.append((c0, cw))
        c0 += cw
    return pl.pallas_call(
        functools.partial(_ffn_body, chunks=tuple(chunks)),
        grid=(n_tok // tm,),
        in_specs=[
            pl.BlockSpec((tm, d), lambda i: (i, 0)),
            _full((1, d)), _full(wg.shape), _full(wu.shape), _full(wd.shape),
        ],
        out_specs=pl.BlockSpec((tm, d), lambda i: (i, 0)),
        out_shape=jax.ShapeDtypeStruct((n_tok, d), F32),
        compiler_params=_params(1),
        name="ffn",
    )(h, g, wg, wu, wd)


def _conv_body(h_ref, g_ref, wa_ref, cw_ref, cb_ref, lng_ref, lnb_ref, wo_ref, o_ref,
               ext_ref, y_ref, *, tm, conv_k):
    ch = o_ref.shape[-1]
    sub = ch // LANES
    halo = CONV_HALO * sub
    s = pl.program_id(1)

    @pl.when(s == 0)
    def _():
        ext_ref[0:halo, :] = jnp.zeros((halo, LANES), F32)

    @pl.when(s > 0)
    def _():
        ext_ref[0:halo, :] = ext_ref[tm * sub:tm * sub + halo, :]

    u = _rms(h_ref[0], g_ref[...]).astype(BF16)
    a = _dot(u, wa_ref[...])
    glu = a[:, :ch] * jax.nn.sigmoid(a[:, ch:])
    for k in range(sub):
        ext_ref[pl.ds(halo + k, tm, stride=sub), :] = glu[:, k * LANES:(k + 1) * LANES]

    off = CONV_HALO - (conv_k - 1)
    taps = [cw_ref[j * sub:(j + 1) * sub, :] for j in range(conv_k)]
    bias = cb_ref[...]

    def steps(r, carry):
        t0 = r * CONV_STEPS
        acc = jnp.broadcast_to(bias, (CONV_STEPS, sub, LANES))
        for j in range(conv_k):
            rows = pl.ds(pl.multiple_of((t0 + off + j) * sub, sub), CONV_STEPS * sub)
            acc = acc + taps[j] * ext_ref[rows, :].reshape(CONV_STEPS, sub, LANES)
        y_ref[pl.ds(pl.multiple_of(t0 * sub, sub), CONV_STEPS * sub), :] = acc.reshape(
            CONV_STEPS * sub, LANES)
        return carry

    lax.fori_loop(0, tm // CONV_STEPS, steps, 0)

    y = jnp.concatenate([y_ref[pl.ds(k, tm, stride=sub), :] for k in range(sub)], axis=-1)
    mu = jnp.mean(y, axis=-1, keepdims=True)
    yc = y - mu
    var = jnp.mean(yc * yc, axis=-1, keepdims=True)
    yn = yc * lax.rsqrt(var + EPS) * lng_ref[...] + lnb_ref[...]
    act = (yn * jax.nn.sigmoid(yn)).astype(BF16)
    o_ref[0] = _dot(act, wo_ref[...])


def _conv_mixer(h3, g, w_in, conv_w, conv_b, ln_g, ln_b, wo, *, tm):
    b, s, d = h3.shape
    ch = wo.shape[0]
    conv_k = conv_w.shape[0]
    sub = ch // LANES
    assert conv_k - 1 <= CONV_HALO and tm % CONV_STEPS == 0 and sub == SUBLANES
    cw = conv_w.reshape(conv_k * sub, LANES)
    cb = conv_b.reshape(sub, LANES)
    return pl.pallas_call(
        functools.partial(_conv_body, tm=tm, conv_k=conv_k),
        grid=(b, s // tm),
        in_specs=[
            pl.BlockSpec((1, tm, d), lambda i, j: (i, j, 0)),
            _full((1, d)), _cols(w_in, 0, 2 * ch), _full(cw.shape), _full(cb.shape),
            _full((1, ch)), _full((1, ch)), _full(wo.shape),
        ],
        out_specs=pl.BlockSpec((1, tm, d), lambda i, j: (i, j, 0)),
        out_shape=jax.ShapeDtypeStruct((b, s, d), F32),
        scratch_shapes=[
            pltpu.VMEM(((tm + CONV_HALO) * sub, LANES), F32),
            pltpu.VMEM((tm * sub, LANES), F32),
        ],
        compiler_params=_params(2),
        name="conv_mixer",
    )(h3, g, w_in, cw, cb, ln_g, ln_b, wo)


def _rotate(t, cos, sin):
    half = t.shape[-1] // 2
    t1, t2 = t[:, :half], t[:, half:]
    return jnp.concatenate([t1 * cos - t2 * sin, t2 * cos + t1 * sin], axis=-1)


def _ret_body(h_ref, g_ref, pos_ref, inv_ref, wq_ref, wk_ref, wv_ref, wg_ref, wo_ref, o_ref,
              state_ref, gated_ref, *, tm, dk, dv):
    heads = RET_HEADS
    c = RET_CHUNK
    s = pl.program_id(1)

    @pl.when(s == 0)
    def _():
        state_ref[...] = jnp.zeros(state_ref.shape, F32)

    u = _rms(h_ref[0], g_ref[...]).astype(BF16)
    q = _dot(u, wq_ref[...])
    k = _dot(u, wk_ref[...])
    v = _dot(u, wv_ref[...])
    gr = _dot(u, wg_ref[...])

    ang = pos_ref[0].astype(F32) * inv_ref[...]
    cos, sin = jnp.cos(ang), jnp.sin(ang)

    row = lax.broadcasted_iota(jnp.int32, (c, c), 0)
    col = lax.broadcasted_iota(jnp.int32, (c, c), 1)
    rel = (row - col).astype(F32)
    idx_q = lax.broadcasted_iota(jnp.int32, (c, dk), 0).astype(F32)

    for hd in range(heads):
        log_g = math.log1p(-(2.0 ** (-5.0 - hd)))
        intra = jnp.where(rel >= 0, jnp.exp(log_g * jnp.maximum(rel, 0.0)), 0.0)
        q_dec = jnp.exp(log_g * (idx_q + 1.0))
        k_dec = jnp.exp(log_g * (c - 1.0 - idx_q))
        chunk_dec = math.exp(log_g * c)
        qh = _rotate(q[:, hd * dk:(hd + 1) * dk], cos, sin)
        kh = _rotate(k[:, hd * dk:(hd + 1) * dk], cos, sin) * (dk ** -0.5)
        for ci in range(tm // c):
            rows = slice(ci * c, (ci + 1) * c)
            qb, kb = qh[rows], kh[rows]
            vb = v[rows, hd * dv:(hd + 1) * dv].astype(BF16)
            state = state_ref[hd]
            scores = lax.dot_general(qb.astype(BF16), kb.astype(BF16),
                                     (((1,), (1,)), ((), ())),
                                     preferred_element_type=F32) * intra
            inner = _dot(scores.astype(BF16), vb)
            cross = _dot((qb * q_dec).astype(BF16), state.astype(BF16))
            state_ref[hd] = chunk_dec * state + lax.dot_general(
                (kb * k_dec).astype(BF16), vb, (((0,), (0,)), ((), ())),
                preferred_element_type=F32)
            o = inner + cross
            o = o * lax.rsqrt(jnp.mean(o * o, axis=-1, keepdims=True) + EPS)
            gg = gr[rows, hd * dv:(hd + 1) * dv]
            gated_ref[rows, hd * dv:(hd + 1) * dv] = (gg * jax.nn.sigmoid(gg) * o).astype(BF16)

    o_ref[0] = _dot(gated_ref[...], wo_ref[...])


def _ret_mixer(h3, g, pos3, inv, w_in, col0, wo, *, tm):
    b, s, d = h3.shape
    dk = d // RET_HEADS
    dv = wo.shape[0] // RET_HEADS
    qk, vw = RET_HEADS * dk, RET_HEADS * dv
    assert tm % RET_CHUNK == 0
    return pl.pallas_call(
        functools.partial(_ret_body, tm=tm, dk=dk, dv=dv),
        grid=(b, s // tm),
        in_specs=[
            pl.BlockSpec((1, tm, d), lambda i, j: (i, j, 0)),
            _full((1, d)),
            pl.BlockSpec((1, tm, 1), lambda i, j: (i, j, 0)),
            _full(inv.shape),
            _cols(w_in, col0, qk), _cols(w_in, col0 + qk, qk),
            _cols(w_in, col0 + 2 * qk, vw), _cols(w_in, col0 + 2 * qk + vw, vw),
            _full(wo.shape),
        ],
        out_specs=pl.BlockSpec((1, tm, d), lambda i, j: (i, j, 0)),
        out_shape=jax.ShapeDtypeStruct((b, s, d), F32),
        scratch_shapes=[
            pltpu.VMEM((RET_HEADS, dk, dv), F32),
            pltpu.VMEM((tm, RET_HEADS * dv), BF16),
        ],
        compiler_params=_params(2),
        name="ret_mixer",
    )(h3, g, pos3, inv, w_in, w_in, w_in, w_in, wo)


def _merge_body(h_ref, ya_ref, yb_ref, g_ref, wg_ref, bg_ref, wo_ref, o_ref):
    x = h_ref[...]
    d = x.shape[-1]
    u = _rms(x, g_ref[...]).astype(BF16)
    zg = _dot(u, wg_ref[...])
    m = (jax.nn.sigmoid(zg[:, :d] + bg_ref[0:1, :]) * ya_ref[...]
         + jax.nn.sigmoid(zg[:, d:] + bg_ref[1:2, :]) * yb_ref[...])
    o_ref[...] = x + _dot(m.astype(BF16), wo_ref[...])


def _merge(h, ya, yb, g, w_in, col0, bg, wo, *, tm):
    n_tok, d = h.shape
    tile = pl.BlockSpec((tm, d), lambda i: (i, 0))
    return pl.pallas_call(
        _merge_body,
        grid=(n_tok // tm,),
        in_specs=[tile, tile, tile, _full((1, d)), _cols(w_in, col0, 2 * d), _full(bg.shape),
                  _full(wo.shape)],
        out_specs=tile,
        out_shape=jax.ShapeDtypeStruct((n_tok, d), F32),
        compiler_params=_params(1),
        name="merge",
    )(h, ya, yb, g, w_in, bg, wo)


def _ple_body(h_ref, p_ref, gh_ref, wpg_ref, wpe_ref, gp_ref, gf_ref, o_ref, *, final):
    x = h_ref[...]
    e = _rms(_dot(p_ref[...].astype(BF16), wpe_ref[...]), gp_ref[...])
    gate = jax.nn.sigmoid(_dot(_rms(x, gh_ref[...]).astype(BF16), wpg_ref[...]))
    out = x + gate * e
    if final:
        out = _rms(out, gf_ref[...])
    o_ref[...] = out


def _ple(h, p, gh, wpg, wpe, gp, gf, *, tm, final):
    n_tok, d = h.shape
    pd = p.shape[1]
    tile = pl.BlockSpec((tm, d), lambda i: (i, 0))
    return pl.pallas_call(
        functools.partial(_ple_body, final=final),
        grid=(n_tok // tm,),
        in_specs=[tile, pl.BlockSpec((tm, pd), lambda i: (i, 0)), _full((1, d)),
                  _full(wpg.shape), _full(wpe.shape), _full((1, d)), _full((1, d))],
        out_specs=tile,
        out_shape=jax.ShapeDtypeStruct((n_tok, d), F32),
        compiler_params=_params(1),
        name="ple",
    )(h, p, gh, wpg, wpe, gp, gf)


def kernel(x, p, positions, ffn1_norm, ffn1_w_gate, ffn1_w_up, ffn1_w_down, mix_norm, w_in,
           b_gates, conv_w, conv_b, conv_ln_g, conv_ln_b, w_conv_o, w_ret_o, w_out, ffn2_norm,
           ffn2_w_gate, ffn2_w_up, ffn2_w_down, ple_norm, w_ple_gate, w_ple_proj,
           ple_proj_norm, final_norm):
    b, s, d = x.shape
    depth = p.shape[0]
    n_tok = b * s
    conv_in = 2 * w_conv_o.shape[1]
    gates_at = w_in.shape[2] - 2 * d

    half = d // RET_HEADS // 2
    inv = (ROPE_BASE ** (-jnp.arange(half, dtype=F32) / half)).reshape(1, half)
    pos3 = positions.reshape(b, s, 1)

    def row(v):
        return v.reshape(1, -1)

    h = x.reshape(n_tok, d)
    for i in range(depth):
        w_in_b = w_in[i].astype(BF16)
        h = _ffn(h, row(ffn1_norm[i]), ffn1_w_gate[i].astype(BF16), ffn1_w_up[i].astype(BF16),
                 ffn1_w_down[i].astype(BF16), tm=512)
        h3 = h.reshape(b, s, d)
        ya = _conv_mixer(h3, row(mix_norm[i]), w_in_b, conv_w[i], conv_b[i],
                         row(conv_ln_g[i]), row(conv_ln_b[i]), w_conv_o[i].astype(BF16), tm=512)
        yb = _ret_mixer(h3, row(mix_norm[i]), pos3, inv, w_in_b, conv_in,
                        w_ret_o[i].astype(BF16), tm=512)
        h = _merge(h, ya.reshape(n_tok, d), yb.reshape(n_tok, d), row(mix_norm[i]),
                   w_in_b, gates_at, b_gates[i], w_out[i].astype(BF16), tm=512)
        h = _ffn(h, row(ffn2_norm[i]), ffn2_w_gate[i].astype(BF16), ffn2_w_up[i].astype(BF16),
                 ffn2_w_down[i].astype(BF16), tm=512)
        h = _ple(h, p[i].reshape(n_tok, -1), row(ple_norm[i]), w_ple_gate[i].astype(BF16),
                 w_ple_proj[i].astype(BF16), row(ple_proj_norm[i]), row(final_norm),
                 tm=512, final=(i == depth - 1))
    return h.reshape(b, s, d)
```

```python
import functools
import math

import jax
import jax.numpy as jnp
from jax import lax
from jax.experimental import pallas as pl
from jax.experimental.pallas import tpu as pltpu

F32 = jnp.float32
BF16 = jnp.bfloat16

EPS = 1e-6
ROPE_BASE = 10000.0
RET_HEADS = 4
RET_CHUNK = 128
CONV_HALO = 32
CONV_STEPS = 8
LANES = 128
SUBLANES = 8
FFN_CHUNK = 768

VMEM_LIMIT_BYTES = 56 * 1024 * 1024


def _params(n_axes):
    return pltpu.CompilerParams(
        dimension_semantics=("arbitrary",) * n_axes,
        vmem_limit_bytes=VMEM_LIMIT_BYTES,
    )


def _rms(x, g):
    return x * lax.rsqrt(jnp.mean(x * x, axis=-1, keepdims=True) + EPS) * g


def _silu(x):
    return x * jax.nn.sigmoid(x)


def _dot(a, b):
    return jnp.dot(a, b, preferred_element_type=F32)


def _full(shape):
    nd = len(shape)
    return pl.BlockSpec(shape, lambda *_: (0,) * nd)


def _cols(w, col0, width):
    assert col0 % width == 0
    return pl.BlockSpec((w.shape[0], width), lambda *_: (0, col0 // width),
                        pipeline_mode=pl.Buffered(1))


def _swiglu_residual(x, g_ref, wg_ref, wu_ref, wd_ref):
    d_ff = wg_ref.shape[1]
    n = _rms(x, g_ref[...]).astype(BF16)
    acc = jnp.zeros(x.shape, F32)
    for c0 in range(0, d_ff, FFN_CHUNK):
        c1 = min(c0 + FFN_CHUNK, d_ff)
        act = (_silu(_dot(n, wg_ref[:, c0:c1])) * _dot(n, wu_ref[:, c0:c1])).astype(BF16)
        acc = acc + _dot(act, wd_ref[c0:c1, :])
    return x + 0.5 * acc


def _ffn_body(h_ref, g_ref, wg_ref, wu_ref, wd_ref, o_ref):
    o_ref[...] = _swiglu_residual(h_ref[...], g_ref, wg_ref, wu_ref, wd_ref)


def _ffn_ple_body(h_ref, g_ref, wg_ref, wu_ref, wd_ref, p_ref, gh_ref, wpg_ref, wpe_ref, gp_ref,
                  gf_ref, o_ref, *, final):
    x = _swiglu_residual(h_ref[...], g_ref, wg_ref, wu_ref, wd_ref)
    e = _rms(_dot(p_ref[...].astype(BF16), wpe_ref[...]), gp_ref[...])
    gate = jax.nn.sigmoid(_dot(_rms(x, gh_ref[...]).astype(BF16), wpg_ref[...]))
    out = x + gate * e
    if final:
        out = _rms(out, gf_ref[...])
    o_ref[...] = out


def _ffn(h, g, wg, wu, wd, *, tm):
    n_tok, d = h.shape
    tile = pl.BlockSpec((tm, d), lambda i: (i, 0))
    return pl.pallas_call(
        _ffn_body,
        grid=(n_tok // tm,),
        in_specs=[tile, _full((1, d)), _full(wg.shape), _full(wu.shape), _full(wd.shape)],
        out_specs=tile,
        out_shape=jax.ShapeDtypeStruct((n_tok, d), F32),
        compiler_params=_params(1),
        name="ffn",
    )(h, g, wg, wu, wd)


def _ffn_ple(h, g, wg, wu, wd, p, gh, wpg, wpe, gp, gf, *, tm, final):
    n_tok, d = h.shape
    pd = p.shape[1]
    tile = pl.BlockSpec((tm, d), lambda i: (i, 0))
    return pl.pallas_call(
        functools.partial(_ffn_ple_body, final=final),
        grid=(n_tok // tm,),
        in_specs=[tile, _full((1, d)), _full(wg.shape), _full(wu.shape), _full(wd.shape),
                  pl.BlockSpec((tm, pd), lambda i: (i, 0)), _full((1, d)), _full(wpg.shape),
                  _full(wpe.shape), _full((1, d)), _full((1, d))],
        out_specs=tile,
        out_shape=jax.ShapeDtypeStruct((n_tok, d), F32),
        compiler_params=_params(1),
        name="ffn_ple",
    )(h, g, wg, wu, wd, p, gh, wpg, wpe, gp, gf)


def _conv_branch(u, wa_ref, cw_ref, cb_ref, lng_ref, lnb_ref, ext_ref, y_ref, *, tm, conv_k):
    ch = wa_ref.shape[1] // 2
    sub = ch // LANES
    halo = CONV_HALO * sub
    a = _dot(u, wa_ref[...])
    glu = a[:, :ch] * jax.nn.sigmoid(a[:, ch:])
    for k in range(sub):
        ext_ref[pl.ds(halo + k, tm, stride=sub), :] = glu[:, k * LANES:(k + 1) * LANES]

    off = CONV_HALO - (conv_k - 1)
    taps = [cw_ref[j * sub:(j + 1) * sub, :] for j in range(conv_k)]
    bias = cb_ref[...]
    for t0 in range(0, tm, CONV_STEPS):
        acc = jnp.broadcast_to(bias, (CONV_STEPS, sub, LANES))
        for j in range(conv_k):
            r0 = (t0 + off + j) * sub
            acc = acc + taps[j] * ext_ref[r0:r0 + CONV_STEPS * sub, :].reshape(CONV_STEPS, sub, LANES)
        y_ref[t0 * sub:(t0 + CONV_STEPS) * sub, :] = acc.reshape(CONV_STEPS * sub, LANES)

    y = jnp.concatenate([y_ref[pl.ds(k, tm, stride=sub), :] for k in range(sub)], axis=-1)
    mu = jnp.mean(y, axis=-1, keepdims=True)
    yc = y - mu
    var = jnp.mean(yc * yc, axis=-1, keepdims=True)
    yn = yc * lax.rsqrt(var + EPS) * lng_ref[...] + lnb_ref[...]
    return _silu(yn).astype(BF16)


def _rotate(t, cos, sin):
    half = t.shape[-1] // 2
    t1, t2 = t[:, :half], t[:, half:]
    return jnp.concatenate([t1 * cos - t2 * sin, t2 * cos + t1 * sin], axis=-1)


def _ret_branch(u, pos, inv_ref, wq_ref, wk_ref, wv_ref, wg_ref, state_ref, gated_ref, *, tm):
    heads = RET_HEADS
    c = RET_CHUNK
    dk = wq_ref.shape[1] // heads
    dv = wv_ref.shape[1] // heads
    q = _dot(u, wq_ref[...])
    k = _dot(u, wk_ref[...])
    v = _dot(u, wv_ref[...]).astype(BF16)
    gr = _dot(u, wg_ref[...])

    ang = pos.astype(F32) * inv_ref[...]
    cos, sin = jnp.cos(ang), jnp.sin(ang)

    row = lax.broadcasted_iota(jnp.int32, (c, c), 0)
    col = lax.broadcasted_iota(jnp.int32, (c, c), 1)
    rel = (row - col).astype(F32)
    idx_q = lax.broadcasted_iota(jnp.int32, (c, dk), 0).astype(F32)

    log_g = [math.log1p(-(2.0 ** (-5.0 - hd))) for hd in range(heads)]
    intra = [jnp.where(rel >= 0, jnp.exp(lg * jnp.maximum(rel, 0.0)), 0.0) for lg in log_g]
    q_dec = [jnp.exp(lg * (idx_q + 1.0)) for lg in log_g]
    k_dec = [jnp.exp(lg * (c - 1.0 - idx_q)) for lg in log_g]
    for ci in range(tm // c):
        rows = slice(ci * c, (ci + 1) * c)
        for hd in range(heads):
            qb = _rotate(q[rows, hd * dk:(hd + 1) * dk], cos[rows], sin[rows])
            kb = _rotate(k[rows, hd * dk:(hd + 1) * dk], cos[rows], sin[rows]) * (dk ** -0.5)
            vb = v[rows, hd * dv:(hd + 1) * dv]
            state = state_ref[hd]
            scores = lax.dot_general(qb.astype(BF16), kb.astype(BF16),
                                     (((1,), (1,)), ((), ())),
                                     preferred_element_type=F32) * intra[hd]
            inner = _dot(scores.astype(BF16), vb)
            cross = _dot((qb * q_dec[hd]).astype(BF16), state.astype(BF16))
            state_ref[hd] = math.exp(log_g[hd] * c) * state + lax.dot_general(
                (kb * k_dec[hd]).astype(BF16), vb, (((0,), (0,)), ((), ())),
                preferred_element_type=F32)
            o = inner + cross
            o = o * lax.rsqrt(jnp.mean(o * o, axis=-1, keepdims=True) + EPS)
            gated_ref[rows, hd * dv:(hd + 1) * dv] = (
                _silu(gr[rows, hd * dv:(hd + 1) * dv]) * o).astype(BF16)

    return gated_ref[...]


def _mixer_body(h_ref, g_ref, pos_ref, inv_ref, wa_ref, wq_ref, wk_ref, wv_ref, wg_ref, wgt_ref,
                bg_ref, cw_ref, cb_ref, lng_ref, lnb_ref, wco_ref, wro_ref, wout_ref, o_ref,
                ext_ref, y_ref, state_ref, gated_ref, *, tm, conv_k):
    d = o_ref.shape[-1]
    halo = CONV_HALO * (wco_ref.shape[0] // LANES)
    s = pl.program_id(1)

    @pl.when(s == 0)
    def _():
        ext_ref[0:halo, :] = jnp.zeros((halo, LANES), F32)
        state_ref[...] = jnp.zeros(state_ref.shape, F32)

    @pl.when(s > 0)
    def _():
        ext_ref[0:halo, :] = ext_ref[pl.ds(ext_ref.shape[0] - halo, halo), :]

    x = h_ref[0]
    u = _rms(x, g_ref[...]).astype(BF16)
    conv_act = _conv_branch(u, wa_ref, cw_ref, cb_ref, lng_ref, lnb_ref, ext_ref, y_ref,
                            tm=tm, conv_k=conv_k)
    zg = _dot(u, wgt_ref[...])
    ret_act = _ret_branch(u, pos_ref[0], inv_ref, wq_ref, wk_ref, wv_ref, wg_ref, state_ref,
                          gated_ref, tm=tm)
    yb = _dot(ret_act, wro_ref[...])
    ya = _dot(conv_act, wco_ref[...])
    m = (jax.nn.sigmoid(zg[:, :d] + bg_ref[0:1, :]) * ya
         + jax.nn.sigmoid(zg[:, d:] + bg_ref[1:2, :]) * yb)
    o_ref[0] = x + _dot(m.astype(BF16), wout_ref[...])


def _mixer(h3, g, pos3, inv, w_in, bg, conv_w, conv_b, ln_g, ln_b, wco, wro, wout, *, tm):
    b, s, d = h3.shape
    ch = wco.shape[0]
    conv_k = conv_w.shape[0]
    sub = ch // LANES
    qk = d
    vw = wro.shape[0]
    dv = vw // RET_HEADS
    assert conv_k - 1 <= CONV_HALO and tm % CONV_STEPS == 0 and sub == SUBLANES
    assert tm % RET_CHUNK == 0 and w_in.shape[1] == 2 * ch + 2 * qk + 2 * vw + 2 * d
    cw = conv_w.reshape(conv_k * sub, LANES)
    cb = conv_b.reshape(sub, LANES)
    c_q = 2 * ch
    c_v = c_q + 2 * qk
    c_gt = c_v + 2 * vw
    tile = pl.BlockSpec((1, tm, d), lambda i, j: (i, j, 0))
    return pl.pallas_call(
        functools.partial(_mixer_body, tm=tm, conv_k=conv_k),
        grid=(b, s // tm),
        in_specs=[
            tile, _full((1, d)), pl.BlockSpec((1, tm, 1), lambda i, j: (i, j, 0)), _full(inv.shape),
            _cols(w_in, 0, 2 * ch), _cols(w_in, c_q, qk), _cols(w_in, c_q + qk, qk),
            _cols(w_in, c_v, vw), _cols(w_in, c_v + vw, vw), _cols(w_in, c_gt, 2 * d),
            _full(bg.shape), _full(cw.shape), _full(cb.shape), _full((1, ch)), _full((1, ch)),
            _full(wco.shape), _full(wro.shape), _full(wout.shape),
        ],
        out_specs=tile,
        out_shape=jax.ShapeDtypeStruct((b, s, d), F32),
        scratch_shapes=[
            pltpu.VMEM(((tm + CONV_HALO) * sub, LANES), F32),
            pltpu.VMEM((tm * sub, LANES), F32),
            pltpu.VMEM((RET_HEADS, qk // RET_HEADS, dv), F32),
            pltpu.VMEM((tm, vw), BF16),
        ],
        compiler_params=_params(2),
        name="mixer",
    )(h3, g, pos3, inv, w_in, w_in, w_in, w_in, w_in, w_in, bg, cw, cb, ln_g, ln_b, wco, wro, wout)


def kernel(x, p, positions, ffn1_norm, ffn1_w_gate, ffn1_w_up, ffn1_w_down, mix_norm, w_in,
           b_gates, conv_w, conv_b, conv_ln_g, conv_ln_b, w_conv_o, w_ret_o, w_out, ffn2_norm,
           ffn2_w_gate, ffn2_w_up, ffn2_w_down, ple_norm, w_ple_gate, w_ple_proj,
           ple_proj_norm, final_norm):
    b, s, d = x.shape
    depth = p.shape[0]
    n_tok = b * s

    half = d // RET_HEADS // 2
    inv = (ROPE_BASE ** (-jnp.arange(half, dtype=F32) / half)).reshape(1, half)
    pos3 = positions.reshape(b, s, 1)

    def row(v):
        return v.reshape(1, -1)

    def bf(w):
        return w.astype(BF16)

    h = x.reshape(n_tok, d)
    for i in range(depth):
        h = _ffn(h, row(ffn1_norm[i]), bf(ffn1_w_gate[i]), bf(ffn1_w_up[i]), bf(ffn1_w_down[i]),
                 tm=1024)
        h = _mixer(h.reshape(b, s, d), row(mix_norm[i]), pos3, inv, bf(w_in[i]), b_gates[i],
                   conv_w[i], conv_b[i], row(conv_ln_g[i]), row(conv_ln_b[i]), bf(w_conv_o[i]),
                   bf(w_ret_o[i]), bf(w_out[i]), tm=256).reshape(n_tok, d)
        h = _ffn_ple(h, row(ffn2_norm[i]), bf(ffn2_w_gate[i]), bf(ffn2_w_up[i]),
                     bf(ffn2_w_down[i]), p[i].reshape(n_tok, -1), row(ple_norm[i]),
                     bf(w_ple_gate[i]), bf(w_ple_proj[i]), row(ple_proj_norm[i]), row(final_norm),
                     tm=512, final=(i == depth - 1))
    return h.reshape(b, s, d)
```

```python
import functools
import math

import jax
import jax.numpy as jnp
from jax import lax
from jax.experimental import pallas as pl
from jax.experimental.pallas import tpu as pltpu

F32 = jnp.float32
BF16 = jnp.bfloat16

EPS = 1e-6
ROPE_BASE = 10000.0
RET_HEADS = 4
CONV_HALO = 32
CONV_STEPS = 8
LANES = 128
SUBLANES = 8
FFN_CHUNK = 768

VMEM_LIMIT_BYTES = 56 * 1024 * 1024


def _params(n_axes):
    return pltpu.CompilerParams(
        dimension_semantics=("arbitrary",) * n_axes,
        vmem_limit_bytes=VMEM_LIMIT_BYTES,
    )


def _rms(x, g):
    return x * lax.rsqrt(jnp.mean(x * x, axis=-1, keepdims=True) + EPS) * g


def _silu(x):
    return x * jax.nn.sigmoid(x)


def _dot(a, b):
    return jnp.dot(a, b, preferred_element_type=F32)


def _full(shape):
    nd = len(shape)
    return pl.BlockSpec(shape, lambda *_: (0,) * nd)


def _layer(w, i):
    return pl.BlockSpec((None,) + w.shape[1:], lambda *_: (i, 0, 0), pipeline_mode=pl.Buffered(1))


def _layer_cols(w, i, col0, width):
    assert col0 % width == 0
    return pl.BlockSpec((None, w.shape[1], width), lambda *_: (i, 0, col0 // width),
                        pipeline_mode=pl.Buffered(1))


def _swiglu_residual(x, g_ref, wg_ref, wu_ref, wd_ref):
    d_ff = wg_ref.shape[1]
    n = _rms(x, g_ref[...]).astype(BF16)
    acc = jnp.zeros(x.shape, F32)
    for c0 in range(0, d_ff, FFN_CHUNK):
        c1 = min(c0 + FFN_CHUNK, d_ff)
        act = (_silu(_dot(n, wg_ref[:, c0:c1])) * _dot(n, wu_ref[:, c0:c1])).astype(BF16)
        acc = acc + _dot(act, wd_ref[c0:c1, :])
    return x + 0.5 * acc


def _ffn_body(h_ref, g_ref, wg_ref, wu_ref, wd_ref, o_ref):
    o_ref[...] = _swiglu_residual(h_ref[...], g_ref, wg_ref, wu_ref, wd_ref)


def _ffn_ple_body(h_ref, g_ref, wg_ref, wu_ref, wd_ref, p_ref, gh_ref, wpg_ref, wpe_ref, gp_ref,
                  gf_ref, o_ref, *, final):
    x = _swiglu_residual(h_ref[...], g_ref, wg_ref, wu_ref, wd_ref)
    e = _rms(_dot(p_ref[...].astype(BF16), wpe_ref[...]), gp_ref[...])
    gate = jax.nn.sigmoid(_dot(_rms(x, gh_ref[...]).astype(BF16), wpg_ref[...]))
    out = x + gate * e
    if final:
        out = _rms(out, gf_ref[...])
    o_ref[...] = out


def _ffn(h, g, wg, wu, wd, layer, *, tm):
    n_tok, d = h.shape
    tile = pl.BlockSpec((tm, d), lambda i: (i, 0))
    return pl.pallas_call(
        _ffn_body,
        grid=(n_tok // tm,),
        in_specs=[tile, _full((1, d)), _layer(wg, layer), _layer(wu, layer), _layer(wd, layer)],
        out_specs=tile,
        out_shape=jax.ShapeDtypeStruct((n_tok, d), F32),
        compiler_params=_params(1),
        name="ffn",
    )(h, g, wg, wu, wd)


def _ffn_ple(h, g, wg, wu, wd, p, gh, wpg, wpe, gp, gf, layer, *, tm, final):
    n_tok, d = h.shape
    pd = p.shape[2]
    tile = pl.BlockSpec((tm, d), lambda i: (i, 0))
    return pl.pallas_call(
        functools.partial(_ffn_ple_body, final=final),
        grid=(n_tok // tm,),
        in_specs=[tile, _full((1, d)), _layer(wg, layer), _layer(wu, layer), _layer(wd, layer),
                  pl.BlockSpec((None, tm, pd), lambda i: (layer, i, 0)), _full((1, d)),
                  _layer(wpg, layer), _layer(wpe, layer), _full((1, d)), _full((1, d))],
        out_specs=tile,
        out_shape=jax.ShapeDtypeStruct((n_tok, d), F32),
        compiler_params=_params(1),
        name="ffn_ple",
    )(h, g, wg, wu, wd, p, gh, wpg, wpe, gp, gf)


def _conv_branch(u, wa_ref, cw_ref, cb_ref, lng_ref, lnb_ref, glu_ref, ext_ref, y_ref, *, tm,
                 conv_k):
    ch = wa_ref.shape[1] // 2
    sub = ch // LANES
    halo = CONV_HALO * sub
    a = _dot(u, wa_ref[...])
    glu = a[:, :ch] * jax.nn.sigmoid(a[:, ch:])
    for k in range(sub):
        glu_ref[:, k * SUBLANES:(k + 1) * SUBLANES, :] = glu[:, k * LANES:(k + 1) * LANES].reshape(
            tm // SUBLANES, SUBLANES, LANES)
    for t in range(tm):
        r, s = divmod(t, SUBLANES)
        ext_ref[halo + t * sub:halo + (t + 1) * sub, :] = glu_ref[r, pl.ds(s, sub, stride=SUBLANES), :]

    off = CONV_HALO - (conv_k - 1)
    taps = [cw_ref[j * sub:(j + 1) * sub, :] for j in range(conv_k)]
    bias = cb_ref[...]
    for t0 in range(0, tm, CONV_STEPS):
        acc = jnp.broadcast_to(bias, (CONV_STEPS, sub, LANES))
        for j in range(conv_k):
            r0 = (t0 + off + j) * sub
            acc = acc + taps[j] * ext_ref[r0:r0 + CONV_STEPS * sub, :].reshape(CONV_STEPS, sub, LANES)
        y_ref[t0 * sub:(t0 + CONV_STEPS) * sub, :] = acc.reshape(CONV_STEPS * sub, LANES)

    y = jnp.concatenate([y_ref[pl.ds(k, tm, stride=sub), :] for k in range(sub)], axis=-1)
    mu = jnp.mean(y, axis=-1, keepdims=True)
    yc = y - mu
    var = jnp.mean(yc * yc, axis=-1, keepdims=True)
    yn = yc * lax.rsqrt(var + EPS) * lng_ref[...] + lnb_ref[...]
    return _silu(yn).astype(BF16)


def _rotate(t, cos, sin):
    half = t.shape[-1] // 2
    t1, t2 = t[:, :half], t[:, half:]
    return jnp.concatenate([t1 * cos - t2 * sin, t2 * cos + t1 * sin], axis=-1)


def _log_gamma(hd):
    return math.log1p(-(2.0 ** (-5.0 - hd)))


def _scale_rows(t, col):
    return jnp.concatenate(
        [t[:, i:i + LANES] * col for i in range(0, t.shape[1], LANES)], axis=-1)


def _decay_tables(intra_ref, qdec_ref, kdec_ref):
    c = intra_ref.shape[1]
    row = lax.broadcasted_iota(jnp.int32, (c, c), 0)
    col = lax.broadcasted_iota(jnp.int32, (c, c), 1)
    rel = (row - col).astype(F32)
    idx = lax.broadcasted_iota(jnp.int32, (c, LANES), 0).astype(F32)
    for hd in range(RET_HEADS):
        lg = _log_gamma(hd)
        intra_ref[hd] = jnp.where(rel >= 0, jnp.exp(lg * jnp.maximum(rel, 0.0)), 0.0)
        qdec_ref[hd] = jnp.exp(lg * (idx + 1.0))
        kdec_ref[hd] = jnp.exp(lg * (c - 1.0 - idx))


def _ret_branch(u, cos, sin, wq_ref, wk_ref, wv_ref, wg_ref, intra_ref, qdec_ref, kdec_ref,
                state_ref, gated_ref):
    heads = range(RET_HEADS)
    c = u.shape[0]
    dk = wq_ref.shape[1] // RET_HEADS
    dv = wv_ref.shape[1] // RET_HEADS
    q = _dot(u, wq_ref[...])
    k = _dot(u, wk_ref[...])
    v = _dot(u, wv_ref[...]).astype(BF16)
    gr = _dot(u, wg_ref[...])

    qb = [_rotate(q[:, hd * dk:(hd + 1) * dk], cos, sin) for hd in heads]
    kb = [_rotate(k[:, hd * dk:(hd + 1) * dk], cos, sin) * (dk ** -0.5) for hd in heads]
    vb = [v[:, hd * dv:(hd + 1) * dv] for hd in heads]
    state = [state_ref[hd] for hd in heads]

    scores = [lax.dot_general(qb[hd].astype(BF16), kb[hd].astype(BF16), (((1,), (1,)), ((), ())),
                              preferred_element_type=F32) for hd in heads]
    cross = [_dot(_scale_rows(qb[hd], qdec_ref[hd]).astype(BF16), state[hd].astype(BF16))
             for hd in heads]
    for hd in heads:
        state_ref[hd] = math.exp(_log_gamma(hd) * c) * state[hd] + lax.dot_general(
            _scale_rows(kb[hd], kdec_ref[hd]).astype(BF16), vb[hd], (((0,), (0,)), ((), ())),
            preferred_element_type=F32)
    inner = [_dot((scores[hd] * intra_ref[hd]).astype(BF16), vb[hd]) for hd in heads]
    for hd in heads:
        o = inner[hd] + cross[hd]
        o = o * lax.rsqrt(jnp.mean(o * o, axis=-1, keepdims=True) + EPS)
        gated_ref[:, hd * dv:(hd + 1) * dv] = (_silu(gr[:, hd * dv:(hd + 1) * dv]) * o).astype(BF16)
    return gated_ref[...]


def _mixer_body(h_ref, g_ref, cos_ref, sin_ref, wa_ref, wq_ref, wk_ref, wv_ref, wg_ref, wgt_ref,
                bg_ref, cw_ref, cb_ref, lng_ref, lnb_ref, wco_ref, wro_ref, wout_ref, o_ref,
                glu_ref, ext_ref, y_ref, state_ref, gated_ref, intra_ref, qdec_ref, kdec_ref,
                *, tm, conv_k):
    d = o_ref.shape[-1]
    halo = CONV_HALO * (wco_ref.shape[0] // LANES)
    s = pl.program_id(1)

    @pl.when(s == 0)
    def _():
        ext_ref[0:halo, :] = jnp.zeros((halo, LANES), F32)
        state_ref[...] = jnp.zeros(state_ref.shape, F32)
        _decay_tables(intra_ref, qdec_ref, kdec_ref)

    @pl.when(s > 0)
    def _():
        ext_ref[0:halo, :] = ext_ref[pl.ds(ext_ref.shape[0] - halo, halo), :]

    x = h_ref[0]
    u = _rms(x, g_ref[...]).astype(BF16)
    conv_act = _conv_branch(u, wa_ref, cw_ref, cb_ref, lng_ref, lnb_ref, glu_ref, ext_ref, y_ref,
                            tm=tm, conv_k=conv_k)
    zg = _dot(u, wgt_ref[...])
    ret_act = _ret_branch(u, cos_ref[0], sin_ref[0], wq_ref, wk_ref, wv_ref, wg_ref, intra_ref,
                          qdec_ref, kdec_ref, state_ref, gated_ref)
    yb = _dot(ret_act, wro_ref[...])
    ya = _dot(conv_act, wco_ref[...])
    m = (jax.nn.sigmoid(zg[:, :d] + bg_ref[0:1, :]) * ya
         + jax.nn.sigmoid(zg[:, d:] + bg_ref[1:2, :]) * yb)
    o_ref[0] = x + _dot(m.astype(BF16), wout_ref[...])


def _mixer(h3, g, cos, sin, w_in, bg, conv_w, conv_b, ln_g, ln_b, wco, wro, wout, layer, *, tm):
    b, s, d = h3.shape
    ch = wco.shape[1]
    conv_k = conv_w.shape[0]
    sub = ch // LANES
    qk = d
    vw = wro.shape[1]
    dk, dv = qk // RET_HEADS, vw // RET_HEADS
    half = cos.shape[-1]
    assert conv_k - 1 <= CONV_HALO and tm % CONV_STEPS == 0 and sub == SUBLANES
    assert w_in.shape[2] == 2 * ch + 2 * qk + 2 * vw + 2 * d and dk == 2 * half
    cw = conv_w.reshape(conv_k * sub, LANES)
    cb = conv_b.reshape(sub, LANES)
    c_q = 2 * ch
    c_v = c_q + 2 * qk
    c_gt = c_v + 2 * vw
    tile = pl.BlockSpec((1, tm, d), lambda i, j: (i, j, 0))
    rope = pl.BlockSpec((1, tm, half), lambda i, j: (i, j, 0))
    return pl.pallas_call(
        functools.partial(_mixer_body, tm=tm, conv_k=conv_k),
        grid=(b, s // tm),
        in_specs=[
            tile, _full((1, d)), rope, rope,
            _layer_cols(w_in, layer, 0, 2 * ch), _layer_cols(w_in, layer, c_q, qk),
            _layer_cols(w_in, layer, c_q + qk, qk), _layer_cols(w_in, layer, c_v, vw),
            _layer_cols(w_in, layer, c_v + vw, vw), _layer_cols(w_in, layer, c_gt, 2 * d),
            _full(bg.shape), _full(cw.shape), _full(cb.shape), _full((1, ch)), _full((1, ch)),
            _layer(wco, layer), _layer(wro, layer), _layer(wout, layer),
        ],
        out_specs=tile,
        out_shape=jax.ShapeDtypeStruct((b, s, d), F32),
        scratch_shapes=[
            pltpu.VMEM((tm // SUBLANES, sub * SUBLANES, LANES), F32),
            pltpu.VMEM(((tm + CONV_HALO) * sub, LANES), F32),
            pltpu.VMEM((tm * sub, LANES), F32),
            pltpu.VMEM((RET_HEADS, dk, dv), F32),
            pltpu.VMEM((tm, vw), BF16),
            pltpu.VMEM((RET_HEADS, tm, tm), F32),
            pltpu.VMEM((RET_HEADS, tm, LANES), F32),
            pltpu.VMEM((RET_HEADS, tm, LANES), F32),
        ],
        compiler_params=_params(2),
        name="mixer",
    )(h3, g, cos, sin, w_in, w_in, w_in, w_in, w_in, w_in, bg, cw, cb, ln_g, ln_b, wco, wro, wout)


def _rope_body(pos_ref, inv_ref, cos_ref, sin_ref):
    ang = pos_ref[...].astype(F32) * inv_ref[...]
    cos_ref[...] = jnp.cos(ang)
    sin_ref[...] = jnp.sin(ang)


def _rope_tables(pos, inv, *, tm):
    n_tok = pos.shape[0]
    half = inv.shape[1]
    out = pl.BlockSpec((tm, half), lambda i: (i, 0))
    return pl.pallas_call(
        _rope_body,
        grid=(n_tok // tm,),
        in_specs=[pl.BlockSpec((tm, 1), lambda i: (i, 0)), _full(inv.shape)],
        out_specs=[out, out],
        out_shape=[jax.ShapeDtypeStruct((n_tok, half), F32)] * 2,
        compiler_params=_params(1),
        name="rope_tables",
    )(pos, inv)


def kernel(x, p, positions, ffn1_norm, ffn1_w_gate, ffn1_w_up, ffn1_w_down, mix_norm, w_in,
           b_gates, conv_w, conv_b, conv_ln_g, conv_ln_b, w_conv_o, w_ret_o, w_out, ffn2_norm,
           ffn2_w_gate, ffn2_w_up, ffn2_w_down, ple_norm, w_ple_gate, w_ple_proj,
           ple_proj_norm, final_norm):
    b, s, d = x.shape
    depth = p.shape[0]
    n_tok = b * s

    half = d // RET_HEADS // 2
    inv = (ROPE_BASE ** (-jnp.arange(half, dtype=F32) / half)).reshape(1, half)
    cos, sin = _rope_tables(positions.reshape(n_tok, 1), inv, tm=1024)
    cos, sin = cos.reshape(b, s, half), sin.reshape(b, s, half)

    def row(v):
        return v.reshape(1, -1)

    w1g, w1u, w1d = ffn1_w_gate.astype(BF16), ffn1_w_up.astype(BF16), ffn1_w_down.astype(BF16)
    w2g, w2u, w2d = ffn2_w_gate.astype(BF16), ffn2_w_up.astype(BF16), ffn2_w_down.astype(BF16)
    w_in_b, wco, wro, wout = (w_in.astype(BF16), w_conv_o.astype(BF16), w_ret_o.astype(BF16),
                              w_out.astype(BF16))
    wpg, wpe = w_ple_gate.astype(BF16), w_ple_proj.astype(BF16)
    p3 = p.reshape(depth, n_tok, -1)

    h = x.reshape(n_tok, d)
    for i in range(depth):
        h = _ffn(h, row(ffn1_norm[i]), w1g, w1u, w1d, i, tm=1024)
        h = _mixer(h.reshape(b, s, d), row(mix_norm[i]), cos, sin, w_in_b, b_gates[i], conv_w[i],
                   conv_b[i], row(conv_ln_g[i]), row(conv_ln_b[i]), wco, wro, wout, i,
                   tm=256).reshape(n_tok, d)
        h = _ffn_ple(h, row(ffn2_norm[i]), w2g, w2u, w2d, p3, row(ple_norm[i]), wpg, wpe,
                     row(ple_proj_norm[i]), row(final_norm), i, tm=512, final=(i == depth - 1))
    return h.reshape(b, s, d)
```

```python
import functools
import math

import jax
import jax.numpy as jnp
from jax import lax
from jax.experimental import pallas as pl
from jax.experimental.pallas import tpu as pltpu

F32 = jnp.float32
BF16 = jnp.bfloat16

EPS = 1e-6
ROPE_BASE = 10000.0
RET_HEADS = 4
CONV_HALO = 32
CONV_STEPS = 8
LANES = 128
SUBLANES = 8
FFN_CHUNK = 768
WEIGHT_BLOCK = 1024
PADDED_BLOCK = WEIGHT_BLOCK + LANES

VMEM_LIMIT_BYTES = 56 * 1024 * 1024


def _params(n_axes):
    return pltpu.CompilerParams(
        dimension_semantics=("arbitrary",) * n_axes,
        vmem_limit_bytes=VMEM_LIMIT_BYTES,
    )


def _rms(x, g):
    return x * lax.rsqrt(jnp.mean(x * x, axis=-1, keepdims=True) + EPS) * g


def _sigmoid(x):
    return jax.nn.sigmoid(x)


def _silu(x):
    return x * _sigmoid(x)


def _dot(a, b):
    return jnp.dot(a, b, preferred_element_type=F32)


def _proj(x, blocks):
    return jnp.concatenate([_dot(x, w[:, :WEIGHT_BLOCK]) for w in blocks], axis=-1)


def _pad_blocks(w):
    layers, k, n = w.shape
    assert n % WEIGHT_BLOCK == 0
    w = w.reshape(layers, k, n // WEIGHT_BLOCK, WEIGHT_BLOCK)
    w = jnp.pad(w, ((0, 0), (0, 0), (0, 0), (0, LANES)))
    return w.reshape(layers, k, -1).astype(BF16)


def _full(shape):
    nd = len(shape)
    return pl.BlockSpec(shape, lambda *_: (0,) * nd)


def _layer(w, i):
    return pl.BlockSpec((None,) + w.shape[1:], lambda *_: (i, 0, 0), pipeline_mode=pl.Buffered(1))


def _layer_cols(w, i, col0, width):
    assert col0 % width == 0
    return pl.BlockSpec((None, w.shape[1], width), lambda *_: (i, 0, col0 // width),
                        pipeline_mode=pl.Buffered(1))


def _swiglu_residual(x, g_ref, wg_ref, wu_ref, wd_ref):
    d_ff = wg_ref.shape[1]
    n = _rms(x, g_ref[...]).astype(BF16)
    acc = jnp.zeros(x.shape, F32)
    for c0 in range(0, d_ff, FFN_CHUNK):
        c1 = min(c0 + FFN_CHUNK, d_ff)
        act = (_silu(_dot(n, wg_ref[:, c0:c1])) * _dot(n, wu_ref[:, c0:c1])).astype(BF16)
        acc = acc + _dot(act, wd_ref[c0:c1, :x.shape[1]])
    return x + 0.5 * acc


def _ffn_body(h_ref, g_ref, wg_ref, wu_ref, wd_ref, o_ref):
    o_ref[...] = _swiglu_residual(h_ref[...], g_ref, wg_ref, wu_ref, wd_ref)


def _ffn_ple_body(h_ref, g_ref, wg_ref, wu_ref, wd_ref, p_ref, gh_ref, wpg_ref, wpe_ref, gp_ref,
                  gf_ref, o_ref, *, final):
    x = _swiglu_residual(h_ref[...], g_ref, wg_ref, wu_ref, wd_ref)
    d = x.shape[1]
    e = _rms(_dot(p_ref[...].astype(BF16), wpe_ref[:, :d]), gp_ref[...])
    gate = _sigmoid(_dot(_rms(x, gh_ref[...]).astype(BF16), wpg_ref[:, :d]))
    out = x + gate * e
    if final:
        out = _rms(out, gf_ref[...])
    o_ref[...] = out


def _ffn(h, g, wg, wu, wd, layer, *, tm):
    n_tok, d = h.shape
    tile = pl.BlockSpec((tm, d), lambda i: (i, 0))
    return pl.pallas_call(
        _ffn_body,
        grid=(n_tok // tm,),
        in_specs=[tile, _full((1, d)), _layer(wg, layer), _layer(wu, layer), _layer(wd, layer)],
        out_specs=tile,
        out_shape=jax.ShapeDtypeStruct((n_tok, d), F32),
        compiler_params=_params(1),
        name="ffn",
    )(h, g, wg, wu, wd)


def _ffn_ple(h, g, wg, wu, wd, p, gh, wpg, wpe, gp, gf, layer, *, tm, final):
    n_tok, d = h.shape
    pd = p.shape[2]
    tile = pl.BlockSpec((tm, d), lambda i: (i, 0))
    return pl.pallas_call(
        functools.partial(_ffn_ple_body, final=final),
        grid=(n_tok // tm,),
        in_specs=[tile, _full((1, d)), _layer(wg, layer), _layer(wu, layer), _layer(wd, layer),
                  pl.BlockSpec((None, tm, pd), lambda i: (layer, i, 0)), _full((1, d)),
                  _layer(wpg, layer), _layer(wpe, layer), _full((1, d)), _full((1, d))],
        out_specs=tile,
        out_shape=jax.ShapeDtypeStruct((n_tok, d), F32),
        compiler_params=_params(1),
        name="ffn_ple",
    )(h, g, wg, wu, wd, p, gh, wpg, wpe, gp, gf)


def _conv_branch(u, wa_refs, cw_ref, cb_ref, lng_ref, lnb_ref, glu_ref, ext_ref, y_ref, *, tm,
                 conv_k):
    a = _proj(u, wa_refs)
    ch = a.shape[1] // 2
    sub = ch // LANES
    halo = CONV_HALO * sub
    glu = a[:, :ch] * _sigmoid(a[:, ch:])
    for k in range(sub):
        glu_ref[:, k * SUBLANES:(k + 1) * SUBLANES, :] = glu[:, k * LANES:(k + 1) * LANES].reshape(
            tm // SUBLANES, SUBLANES, LANES)
    for t in range(tm):
        r, s = divmod(t, SUBLANES)
        ext_ref[halo + t * sub:halo + (t + 1) * sub, :] = glu_ref[r, pl.ds(s, sub, stride=SUBLANES), :]

    off = CONV_HALO - (conv_k - 1)
    taps = [cw_ref[j * sub:(j + 1) * sub, :] for j in range(conv_k)]
    bias = cb_ref[...]
    for t0 in range(0, tm, CONV_STEPS):
        acc = jnp.broadcast_to(bias, (CONV_STEPS, sub, LANES))
        for j in range(conv_k):
            r0 = (t0 + off + j) * sub
            acc = acc + taps[j] * ext_ref[r0:r0 + CONV_STEPS * sub, :].reshape(CONV_STEPS, sub, LANES)
        y_ref[t0 * sub:(t0 + CONV_STEPS) * sub, :] = acc.reshape(CONV_STEPS * sub, LANES)

    y = jnp.concatenate([y_ref[pl.ds(k, tm, stride=sub), :] for k in range(sub)], axis=-1)
    mu = jnp.mean(y, axis=-1, keepdims=True)
    yc = y - mu
    var = jnp.mean(yc * yc, axis=-1, keepdims=True)
    yn = yc * lax.rsqrt(var + EPS) * lng_ref[...] + lnb_ref[...]
    return _silu(yn).astype(BF16)


def _rotate(t, cos, sin):
    half = t.shape[-1] // 2
    t1, t2 = t[:, :half], t[:, half:]
    return jnp.concatenate([t1 * cos - t2 * sin, t2 * cos + t1 * sin], axis=-1)


def _log_gamma(hd):
    return math.log1p(-(2.0 ** (-5.0 - hd)))


def _scale_rows(t, col):
    return jnp.concatenate(
        [t[:, i:i + LANES] * col for i in range(0, t.shape[1], LANES)], axis=-1)


def _decay_tables(intra_ref, qdec_ref, kdec_ref):
    c = intra_ref.shape[1]
    row = lax.broadcasted_iota(jnp.int32, (c, c), 0)
    col = lax.broadcasted_iota(jnp.int32, (c, c), 1)
    rel = (row - col).astype(F32)
    idx = lax.broadcasted_iota(jnp.int32, (c, LANES), 0).astype(F32)
    for hd in range(RET_HEADS):
        lg = _log_gamma(hd)
        intra_ref[hd] = jnp.where(rel >= 0, jnp.exp(lg * jnp.maximum(rel, 0.0)), 0.0)
        qdec_ref[hd] = jnp.exp(lg * (idx + 1.0))
        kdec_ref[hd] = jnp.exp(lg * (c - 1.0 - idx))


def _ret_stage1(q, k, v, cos, sin, qdec_ref, kdec_ref, state_ref):
    heads = range(RET_HEADS)
    c = q.shape[0]
    dk = q.shape[1] // RET_HEADS
    dv = v.shape[1] // RET_HEADS
    qb = [_rotate(q[:, hd * dk:(hd + 1) * dk], cos, sin) for hd in heads]
    kb = [_rotate(k[:, hd * dk:(hd + 1) * dk], cos, sin) * (dk ** -0.5) for hd in heads]
    vb = [v[:, hd * dv:(hd + 1) * dv] for hd in heads]
    state = [state_ref[hd] for hd in heads]
    scores = [lax.dot_general(qb[hd].astype(BF16), kb[hd].astype(BF16), (((1,), (1,)), ((), ())),
                              preferred_element_type=F32) for hd in heads]
    cross = [_dot(_scale_rows(qb[hd], qdec_ref[hd]).astype(BF16), state[hd].astype(BF16))
             for hd in heads]
    for hd in heads:
        state_ref[hd] = math.exp(_log_gamma(hd) * c) * state[hd] + lax.dot_general(
            _scale_rows(kb[hd], kdec_ref[hd]).astype(BF16), vb[hd], (((0,), (0,)), ((), ())),
            preferred_element_type=F32)
    return scores, cross, vb


def _ret_stage2(scores, vb, intra_ref):
    return [_dot((s * intra_ref[hd]).astype(BF16), vb[hd]) for hd, s in enumerate(scores)]


def _ret_gate(inner, cross, gr, gated_ref):
    dv = gr.shape[1] // RET_HEADS
    for hd in range(RET_HEADS):
        o = inner[hd] + cross[hd]
        o = o * lax.rsqrt(jnp.mean(o * o, axis=-1, keepdims=True) + EPS)
        gated_ref[:, hd * dv:(hd + 1) * dv] = (_silu(gr[:, hd * dv:(hd + 1) * dv]) * o).astype(BF16)
    return gated_ref[...]


def _mixer_body(h_ref, g_ref, cos_ref, sin_ref, *refs, tm, conv_k, groups):
    blocks, n_in = [], 0
    for n in groups:
        blocks.append(refs[n_in:n_in + n])
        n_in += n
    wa, wq, wk, wv, wg, wgt = blocks
    (bg_ref, cw_ref, cb_ref, lng_ref, lnb_ref, wco_ref, wro_ref, wout_ref, o_ref, glu_ref, ext_ref,
     y_ref, state_ref, gated_ref, intra_ref, qdec_ref, kdec_ref) = refs[n_in:]
    d = o_ref.shape[-1]
    halo = CONV_HALO * (wco_ref.shape[0] // LANES)
    s = pl.program_id(1)

    @pl.when(s == 0)
    def _():
        ext_ref[0:halo, :] = jnp.zeros((halo, LANES), F32)
        state_ref[...] = jnp.zeros(state_ref.shape, F32)
        _decay_tables(intra_ref, qdec_ref, kdec_ref)

    @pl.when(s > 0)
    def _():
        ext_ref[0:halo, :] = ext_ref[pl.ds(ext_ref.shape[0] - halo, halo), :]

    x = h_ref[0]
    u = _rms(x, g_ref[...]).astype(BF16)
    conv_act = _conv_branch(u, wa, cw_ref, cb_ref, lng_ref, lnb_ref, glu_ref, ext_ref, y_ref,
                            tm=tm, conv_k=conv_k)
    zg = _proj(u, wgt)
    q, k, v, gr = _proj(u, wq), _proj(u, wk), _proj(u, wv).astype(BF16), _proj(u, wg)
    scores, cross, vb = _ret_stage1(q, k, v, cos_ref[0], sin_ref[0], qdec_ref, kdec_ref, state_ref)
    inner = _ret_stage2(scores, vb, intra_ref)
    yb = _dot(_ret_gate(inner, cross, gr, gated_ref), wro_ref[:, :d])
    ya = _dot(conv_act, wco_ref[:, :d])
    m = (_sigmoid(zg[:, :d] + bg_ref[0:1, :]) * ya
         + _sigmoid(zg[:, d:] + bg_ref[1:2, :]) * yb)
    o_ref[0] = x + _dot(m.astype(BF16), wout_ref[:, :d])


def _mixer(h3, g, cos, sin, w_in, bg, conv_w, conv_b, ln_g, ln_b, wco, wro, wout, layer, *, tm):
    b, s, d = h3.shape
    ch = wco.shape[1]
    conv_k = conv_w.shape[0]
    sub = ch // LANES
    qk = d
    vw = wro.shape[1]
    dk, dv = qk // RET_HEADS, vw // RET_HEADS
    half = cos.shape[-1]
    widths = (2 * ch, qk, qk, vw, vw, 2 * d)
    groups = tuple(w // WEIGHT_BLOCK for w in widths)
    assert conv_k - 1 <= CONV_HALO and tm % CONV_STEPS == 0 and sub == SUBLANES
    assert all(w % WEIGHT_BLOCK == 0 for w in widths) and d == WEIGHT_BLOCK and dk == 2 * half
    assert w_in.shape[2] == sum(groups) * PADDED_BLOCK
    cw = conv_w.reshape(conv_k * sub, LANES)
    cb = conv_b.reshape(sub, LANES)
    tile = pl.BlockSpec((1, tm, d), lambda i, j: (i, j, 0))
    rope = pl.BlockSpec((1, tm, half), lambda i, j: (i, j, 0))
    in_blocks = [_layer_cols(w_in, layer, j * PADDED_BLOCK, PADDED_BLOCK) for j in range(sum(groups))]
    return pl.pallas_call(
        functools.partial(_mixer_body, tm=tm, conv_k=conv_k, groups=groups),
        grid=(b, s // tm),
        in_specs=[
            tile, _full((1, d)), rope, rope, *in_blocks,
            _full(bg.shape), _full(cw.shape), _full(cb.shape), _full((1, ch)), _full((1, ch)),
            _layer(wco, layer), _layer(wro, layer), _layer(wout, layer),
        ],
        out_specs=tile,
        out_shape=jax.ShapeDtypeStruct((b, s, d), F32),
        scratch_shapes=[
            pltpu.VMEM((tm // SUBLANES, sub * SUBLANES, LANES), F32),
            pltpu.VMEM(((tm + CONV_HALO) * sub, LANES), F32),
            pltpu.VMEM((tm * sub, LANES), F32),
            pltpu.VMEM((RET_HEADS, dk, dv), F32),
            pltpu.VMEM((tm, vw), BF16),
            pltpu.VMEM((RET_HEADS, tm, tm), F32),
            pltpu.VMEM((RET_HEADS, tm, LANES), F32),
            pltpu.VMEM((RET_HEADS, tm, LANES), F32),
        ],
        compiler_params=_params(2),
        name="mixer",
    )(h3, g, cos, sin, *([w_in] * sum(groups)), bg, cw, cb, ln_g, ln_b, wco, wro, wout)


def _rope_body(pos_ref, inv_ref, cos_ref, sin_ref):
    ang = pos_ref[...].astype(F32) * inv_ref[...]
    cos_ref[...] = jnp.cos(ang)
    sin_ref[...] = jnp.sin(ang)


def _rope_tables(pos, inv, *, tm):
    n_tok = pos.shape[0]
    half = inv.shape[1]
    out = pl.BlockSpec((tm, half), lambda i: (i, 0))
    return pl.pallas_call(
        _rope_body,
        grid=(n_tok // tm,),
        in_specs=[pl.BlockSpec((tm, 1), lambda i: (i, 0)), _full(inv.shape)],
        out_specs=[out, out],
        out_shape=[jax.ShapeDtypeStruct((n_tok, half), F32)] * 2,
        compiler_params=_params(1),
        name="rope_tables",
    )(pos, inv)


def kernel(x, p, positions, ffn1_norm, ffn1_w_gate, ffn1_w_up, ffn1_w_down, mix_norm, w_in,
           b_gates, conv_w, conv_b, conv_ln_g, conv_ln_b, w_conv_o, w_ret_o, w_out, ffn2_norm,
           ffn2_w_gate, ffn2_w_up, ffn2_w_down, ple_norm, w_ple_gate, w_ple_proj,
           ple_proj_norm, final_norm):
    b, s, d = x.shape
    depth = p.shape[0]
    n_tok = b * s

    half = d // RET_HEADS // 2
    inv = (ROPE_BASE ** (-jnp.arange(half, dtype=F32) / half)).reshape(1, half)
    cos, sin = _rope_tables(positions.reshape(n_tok, 1), inv, tm=1024)
    cos, sin = cos.reshape(b, s, half), sin.reshape(b, s, half)

    def row(v):
        return v.reshape(1, -1)

    w1g, w1u, w1d = ffn1_w_gate.astype(BF16), ffn1_w_up.astype(BF16), _pad_blocks(ffn1_w_down)
    w2g, w2u, w2d = ffn2_w_gate.astype(BF16), ffn2_w_up.astype(BF16), _pad_blocks(ffn2_w_down)
    w_in_b, wco, wro, wout = (_pad_blocks(w_in), _pad_blocks(w_conv_o), _pad_blocks(w_ret_o),
                              _pad_blocks(w_out))
    wpg, wpe = _pad_blocks(w_ple_gate), _pad_blocks(w_ple_proj)
    p3 = p.reshape(depth, n_tok, -1)

    h = x.reshape(n_tok, d)
    for i in range(depth):
        h = _ffn(h, row(ffn1_norm[i]), w1g, w1u, w1d, i, tm=1024)
        h = _mixer(h.reshape(b, s, d), row(mix_norm[i]), cos, sin, w_in_b, b_gates[i], conv_w[i],
                   conv_b[i], row(conv_ln_g[i]), row(conv_ln_b[i]), wco, wro, wout, i,
                   tm=256).reshape(n_tok, d)
        h = _ffn_ple(h, row(ffn2_norm[i]), w2g, w2u, w2d, p3, row(ple_norm[i]), wpg, wpe,
                     row(ple_proj_norm[i]), row(final_norm), i, tm=512, final=(i == depth - 1))
    return h.reshape(b, s, d)
```

```python
import functools
import math

import jax
import jax.numpy as jnp
from jax import lax
from jax.experimental import pallas as pl
from jax.experimental.pallas import tpu as pltpu

F32 = jnp.float32
BF16 = jnp.bfloat16

EPS = 1e-6
ROPE_BASE = 10000.0
RET_HEADS = 4
CONV_HALO = 32
CONV_STEPS = 8
LANES = 128
SUBLANES = 8
FFN_CHUNK = 768

VMEM_LIMIT_BYTES = 56 * 1024 * 1024


def _params(n_axes):
    return pltpu.CompilerParams(
        dimension_semantics=("arbitrary",) * n_axes,
        vmem_limit_bytes=VMEM_LIMIT_BYTES,
    )


def _rms(x, g):
    return x * lax.rsqrt(jnp.mean(x * x, axis=-1, keepdims=True) + EPS) * g


def _silu(x):
    return x * jax.nn.sigmoid(x)


def _dot(a, b):
    return jnp.dot(a, b, preferred_element_type=F32)


def _full(shape):
    nd = len(shape)
    return pl.BlockSpec(shape, lambda *_: (0,) * nd)


def _layer(w, i):
    return pl.BlockSpec((None,) + w.shape[1:], lambda *_: (i, 0, 0), pipeline_mode=pl.Buffered(1))


def _layer_cols(w, i, col0, width):
    assert col0 % width == 0
    return pl.BlockSpec((None, w.shape[1], width), lambda *_: (i, 0, col0 // width),
                        pipeline_mode=pl.Buffered(1))


def _swiglu_residual(x, g_ref, wg_ref, wu_ref, wd_ref):
    d_ff = wg_ref.shape[1]
    n = _rms(x, g_ref[...]).astype(BF16)
    acc = jnp.zeros(x.shape, F32)
    for c0 in range(0, d_ff, FFN_CHUNK):
        c1 = min(c0 + FFN_CHUNK, d_ff)
        act = (_silu(_dot(n, wg_ref[:, c0:c1])) * _dot(n, wu_ref[:, c0:c1])).astype(BF16)
        acc = acc + _dot(act, wd_ref[c0:c1, :])
    return x + 0.5 * acc


def _ffn_body(h_ref, g_ref, wg_ref, wu_ref, wd_ref, o_ref):
    o_ref[...] = _swiglu_residual(h_ref[...], g_ref, wg_ref, wu_ref, wd_ref)


def _ffn_ple_body(h_ref, g_ref, wg_ref, wu_ref, wd_ref, p_ref, gh_ref, wpg_ref, wpe_ref, gp_ref,
                  gf_ref, o_ref, *, final):
    x = _swiglu_residual(h_ref[...], g_ref, wg_ref, wu_ref, wd_ref)
    e = _rms(_dot(p_ref[...].astype(BF16), wpe_ref[...]), gp_ref[...])
    gate = jax.nn.sigmoid(_dot(_rms(x, gh_ref[...]).astype(BF16), wpg_ref[...]))
    out = x + gate * e
    if final:
        out = _rms(out, gf_ref[...])
    o_ref[...] = out


def _ffn(h, g, wg, wu, wd, layer, *, tm):
    n_tok, d = h.shape
    tile = pl.BlockSpec((tm, d), lambda i: (i, 0))
    return pl.pallas_call(
        _ffn_body,
        grid=(n_tok // tm,),
        in_specs=[tile, _full((1, d)), _layer(wg, layer), _layer(wu, layer), _layer(wd, layer)],
        out_specs=tile,
        out_shape=jax.ShapeDtypeStruct((n_tok, d), F32),
        compiler_params=_params(1),
        name="ffn",
    )(h, g, wg, wu, wd)


def _ffn_ple(h, g, wg, wu, wd, p, gh, wpg, wpe, gp, gf, layer, *, tm, final):
    n_tok, d = h.shape
    pd = p.shape[2]
    tile = pl.BlockSpec((tm, d), lambda i: (i, 0))
    return pl.pallas_call(
        functools.partial(_ffn_ple_body, final=final),
        grid=(n_tok // tm,),
        in_specs=[tile, _full((1, d)), _layer(wg, layer), _layer(wu, layer), _layer(wd, layer),
                  pl.BlockSpec((None, tm, pd), lambda i: (layer, i, 0)), _full((1, d)),
                  _layer(wpg, layer), _layer(wpe, layer), _full((1, d)), _full((1, d))],
        out_specs=tile,
        out_shape=jax.ShapeDtypeStruct((n_tok, d), F32),
        compiler_params=_params(1),
        name="ffn_ple",
    )(h, g, wg, wu, wd, p, gh, wpg, wpe, gp, gf)


def _conv_branch(u, wa_ref, cw_ref, cb_ref, lng_ref, lnb_ref, glu_ref, ext_ref, y_ref, *, tm,
                 conv_k):
    ch = wa_ref.shape[1] // 2
    sub = ch // LANES
    halo = CONV_HALO * sub
    a = _dot(u, wa_ref[...])
    glu = a[:, :ch] * jax.nn.sigmoid(a[:, ch:])
    for k in range(sub):
        glu_ref[:, k * SUBLANES:(k + 1) * SUBLANES, :] = glu[:, k * LANES:(k + 1) * LANES].reshape(
            tm // SUBLANES, SUBLANES, LANES)
    for t in range(tm):
        r, s = divmod(t, SUBLANES)
        ext_ref[halo + t * sub:halo + (t + 1) * sub, :] = glu_ref[r, pl.ds(s, sub, stride=SUBLANES), :]

    off = CONV_HALO - (conv_k - 1)
    taps = [cw_ref[j * sub:(j + 1) * sub, :] for j in range(conv_k)]
    bias = cb_ref[...]
    for t0 in range(0, tm, CONV_STEPS):
        acc = jnp.broadcast_to(bias, (CONV_STEPS, sub, LANES))
        for j in range(conv_k):
            r0 = (t0 + off + j) * sub
            acc = acc + taps[j] * ext_ref[r0:r0 + CONV_STEPS * sub, :].reshape(CONV_STEPS, sub, LANES)
        y_ref[t0 * sub:(t0 + CONV_STEPS) * sub, :] = acc.reshape(CONV_STEPS * sub, LANES)

    y = jnp.concatenate([y_ref[pl.ds(k, tm, stride=sub), :] for k in range(sub)], axis=-1)
    mu = jnp.mean(y, axis=-1, keepdims=True)
    yc = y - mu
    var = jnp.mean(yc * yc, axis=-1, keepdims=True)
    yn = yc * lax.rsqrt(var + EPS) * lng_ref[...] + lnb_ref[...]
    return _silu(yn).astype(BF16)


def _rotate(t, cos, sin):
    half = t.shape[-1] // 2
    t1, t2 = t[:, :half], t[:, half:]
    return jnp.concatenate([t1 * cos - t2 * sin, t2 * cos + t1 * sin], axis=-1)


def _log_gamma(hd):
    return math.log1p(-(2.0 ** (-5.0 - hd)))


def _scale_rows(t, col):
    return jnp.concatenate(
        [t[:, i:i + LANES] * col for i in range(0, t.shape[1], LANES)], axis=-1)


def _decay_tables(intra_ref, qdec_ref, kdec_ref):
    c = intra_ref.shape[1]
    row = lax.broadcasted_iota(jnp.int32, (c, c), 0)
    col = lax.broadcasted_iota(jnp.int32, (c, c), 1)
    rel = (row - col).astype(F32)
    idx = lax.broadcasted_iota(jnp.int32, (c, LANES), 0).astype(F32)
    for hd in range(RET_HEADS):
        lg = _log_gamma(hd)
        intra_ref[hd] = jnp.where(rel >= 0, jnp.exp(lg * jnp.maximum(rel, 0.0)), 0.0)
        qdec_ref[hd] = jnp.exp(lg * (idx + 1.0))
        kdec_ref[hd] = jnp.exp(lg * (c - 1.0 - idx))


def _ret_branch(u, cos, sin, wq_ref, wk_ref, wv_ref, wg_ref, intra_ref, qdec_ref, kdec_ref,
                state_ref, gated_ref):
    heads = range(RET_HEADS)
    c = u.shape[0]
    dk = wq_ref.shape[1] // RET_HEADS
    dv = wv_ref.shape[1] // RET_HEADS
    q = _dot(u, wq_ref[...])
    k = _dot(u, wk_ref[...])
    v = _dot(u, wv_ref[...]).astype(BF16)
    gr = _dot(u, wg_ref[...])

    qb = [_rotate(q[:, hd * dk:(hd + 1) * dk], cos, sin) for hd in heads]
    kb = [_rotate(k[:, hd * dk:(hd + 1) * dk], cos, sin) * (dk ** -0.5) for hd in heads]
    vb = [v[:, hd * dv:(hd + 1) * dv] for hd in heads]
    state = [state_ref[hd] for hd in heads]

    scores = [lax.dot_general(qb[hd].astype(BF16), kb[hd].astype(BF16), (((1,), (1,)), ((), ())),
                              preferred_element_type=F32) for hd in heads]
    cross = [_dot(_scale_rows(qb[hd], qdec_ref[hd]).astype(BF16), state[hd].astype(BF16))
             for hd in heads]
    for hd in heads:
        state_ref[hd] = math.exp(_log_gamma(hd) * c) * state[hd] + lax.dot_general(
            _scale_rows(kb[hd], kdec_ref[hd]).astype(BF16), vb[hd], (((0,), (0,)), ((), ())),
            preferred_element_type=F32)
    inner = [_dot((scores[hd] * intra_ref[hd]).astype(BF16), vb[hd]) for hd in heads]
    for hd in heads:
        o = inner[hd] + cross[hd]
        o = o * lax.rsqrt(jnp.mean(o * o, axis=-1, keepdims=True) + EPS)
        gated_ref[:, hd * dv:(hd + 1) * dv] = (_silu(gr[:, hd * dv:(hd + 1) * dv]) * o).astype(BF16)
    return gated_ref[...]


def _mixer_body(h_ref, g_ref, cos_ref, sin_ref, wa_ref, wq_ref, wk_ref, wv_ref, wg_ref, wgt_ref,
                bg_ref, cw_ref, cb_ref, lng_ref, lnb_ref, wco_ref, wro_ref, wout_ref, o_ref,
                glu_ref, ext_ref, y_ref, state_ref, gated_ref, intra_ref, qdec_ref, kdec_ref,
                *, tm, conv_k):
    d = o_ref.shape[-1]
    halo = CONV_HALO * (wco_ref.shape[0] // LANES)
    s = pl.program_id(1)

    @pl.when(s == 0)
    def _():
        ext_ref[0:halo, :] = jnp.zeros((halo, LANES), F32)
        state_ref[...] = jnp.zeros(state_ref.shape, F32)
        _decay_tables(intra_ref, qdec_ref, kdec_ref)

    @pl.when(s > 0)
    def _():
        ext_ref[0:halo, :] = ext_ref[pl.ds(ext_ref.shape[0] - halo, halo), :]

    x = h_ref[0]
    u = _rms(x, g_ref[...]).astype(BF16)
    conv_act = _conv_branch(u, wa_ref, cw_ref, cb_ref, lng_ref, lnb_ref, glu_ref, ext_ref, y_ref,
                            tm=tm, conv_k=conv_k)
    zg = _dot(u, wgt_ref[...])
    ret_act = _ret_branch(u, cos_ref[0], sin_ref[0], wq_ref, wk_ref, wv_ref, wg_ref, intra_ref,
                          qdec_ref, kdec_ref, state_ref, gated_ref)
    yb = _dot(ret_act, wro_ref[...])
    ya = _dot(conv_act, wco_ref[...])
    m = (jax.nn.sigmoid(zg[:, :d] + bg_ref[0:1, :]) * ya
         + jax.nn.sigmoid(zg[:, d:] + bg_ref[1:2, :]) * yb)
    o_ref[0] = x + _dot(m.astype(BF16), wout_ref[...])


def _mixer(h3, g, cos, sin, w_in, bg, conv_w, conv_b, ln_g, ln_b, wco, wro, wout, layer, *, tm):
    b, s, d = h3.shape
    ch = wco.shape[1]
    conv_k = conv_w.shape[0]
    sub = ch // LANES
    qk = d
    vw = wro.shape[1]
    dk, dv = qk // RET_HEADS, vw // RET_HEADS
    half = cos.shape[-1]
    assert conv_k - 1 <= CONV_HALO and tm % CONV_STEPS == 0 and sub == SUBLANES
    assert w_in.shape[2] == 2 * ch + 2 * qk + 2 * vw + 2 * d and dk == 2 * half
    cw = conv_w.reshape(conv_k * sub, LANES)
    cb = conv_b.reshape(sub, LANES)
    c_q = 2 * ch
    c_v = c_q + 2 * qk
    c_gt = c_v + 2 * vw
    tile = pl.BlockSpec((1, tm, d), lambda i, j: (i, j, 0))
    rope = pl.BlockSpec((1, tm, half), lambda i, j: (i, j, 0))
    return pl.pallas_call(
        functools.partial(_mixer_body, tm=tm, conv_k=conv_k),
        grid=(b, s // tm),
        in_specs=[
            tile, _full((1, d)), rope, rope,
            _layer_cols(w_in, layer, 0, 2 * ch), _layer_cols(w_in, layer, c_q, qk),
            _layer_cols(w_in, layer, c_q + qk, qk), _layer_cols(w_in, layer, c_v, vw),
            _layer_cols(w_in, layer, c_v + vw, vw), _layer_cols(w_in, layer, c_gt, 2 * d),
            _full(bg.shape), _full(cw.shape), _full(cb.shape), _full((1, ch)), _full((1, ch)),
            _layer(wco, layer), _layer(wro, layer), _layer(wout, layer),
        ],
        out_specs=tile,
        out_shape=jax.ShapeDtypeStruct((b, s, d), F32),
        scratch_shapes=[
            pltpu.VMEM((tm // SUBLANES, sub * SUBLANES, LANES), F32),
            pltpu.VMEM(((tm + CONV_HALO) * sub, LANES), F32),
            pltpu.VMEM((tm * sub, LANES), F32),
            pltpu.VMEM((RET_HEADS, dk, dv), F32),
            pltpu.VMEM((tm, vw), BF16),
            pltpu.VMEM((RET_HEADS, tm, tm), F32),
            pltpu.VMEM((RET_HEADS, tm, LANES), F32),
            pltpu.VMEM((RET_HEADS, tm, LANES), F32),
        ],
        compiler_params=_params(2),
        name="mixer",
    )(h3, g, cos, sin, w_in, w_in, w_in, w_in, w_in, w_in, bg, cw, cb, ln_g, ln_b, wco, wro, wout)


def _rope_body(pos_ref, inv_ref, cos_ref, sin_ref):
    ang = pos_ref[...].astype(F32) * inv_ref[...]
    cos_ref[...] = jnp.cos(ang)
    sin_ref[...] = jnp.sin(ang)


def _rope_tables(pos, inv, *, tm):
    n_tok = pos.shape[0]
    half = inv.shape[1]
    out = pl.BlockSpec((tm, half), lambda i: (i, 0))
    return pl.pallas_call(
        _rope_body,
        grid=(n_tok // tm,),
        in_specs=[pl.BlockSpec((tm, 1), lambda i: (i, 0)), _full(inv.shape)],
        out_specs=[out, out],
        out_shape=[jax.ShapeDtypeStruct((n_tok, half), F32)] * 2,
        compiler_params=_params(1),
        name="rope_tables",
    )(pos, inv)


def kernel(x, p, positions, ffn1_norm, ffn1_w_gate, ffn1_w_up, ffn1_w_down, mix_norm, w_in,
           b_gates, conv_w, conv_b, conv_ln_g, conv_ln_b, w_conv_o, w_ret_o, w_out, ffn2_norm,
           ffn2_w_gate, ffn2_w_up, ffn2_w_down, ple_norm, w_ple_gate, w_ple_proj,
           ple_proj_norm, final_norm):
    b, s, d = x.shape
    depth = p.shape[0]
    n_tok = b * s

    half = d // RET_HEADS // 2
    inv = (ROPE_BASE ** (-jnp.arange(half, dtype=F32) / half)).reshape(1, half)
    cos, sin = _rope_tables(positions.reshape(n_tok, 1), inv, tm=1024)
    cos, sin = cos.reshape(b, s, half), sin.reshape(b, s, half)

    def row(v):
        return v.reshape(1, -1)

    w1g, w1u, w1d = ffn1_w_gate.astype(BF16), ffn1_w_up.astype(BF16), ffn1_w_down.astype(BF16)
    w2g, w2u, w2d = ffn2_w_gate.astype(BF16), ffn2_w_up.astype(BF16), ffn2_w_down.astype(BF16)
    w_in_b, wco, wro, wout = (w_in.astype(BF16), w_conv_o.astype(BF16), w_ret_o.astype(BF16),
                              w_out.astype(BF16))
    wpg, wpe = w_ple_gate.astype(BF16), w_ple_proj.astype(BF16)
    p3 = p.reshape(depth, n_tok, -1)

    h = x.reshape(n_tok, d)
    for i in range(depth):
        h = _ffn(h, row(ffn1_norm[i]), w1g, w1u, w1d, i, tm=1024)
        h = _mixer(h.reshape(b, s, d), row(mix_norm[i]), cos, sin, w_in_b, b_gates[i], conv_w[i],
                   conv_b[i], row(conv_ln_g[i]), row(conv_ln_b[i]), wco, wro, wout, i,
                   tm=256).reshape(n_tok, d)
        h = _ffn_ple(h, row(ffn2_norm[i]), w2g, w2u, w2d, p3, row(ple_norm[i]), wpg, wpe,
                     row(ple_proj_norm[i]), row(final_norm), i, tm=1024, final=(i == depth - 1))
    return h.reshape(b, s, d)
```

```python
import functools
import math

import jax
import jax.numpy as jnp
from jax import lax
from jax.experimental import pallas as pl
from jax.experimental.pallas import tpu as pltpu

F32 = jnp.float32
BF16 = jnp.bfloat16

EPS = 1e-6
ROPE_BASE = 10000.0
RET_HEADS = 4
CONV_HALO = 32
CONV_STEPS = 16
LANES = 128
SUBLANES = 8
FFN_CHUNK = 768
WEIGHT_BLOCK = 1024
WINDOW = WEIGHT_BLOCK + LANES

VMEM_LIMIT_BYTES = 56 * 1024 * 1024


def _params(n_axes):
    return pltpu.CompilerParams(
        dimension_semantics=("arbitrary",) * n_axes,
        vmem_limit_bytes=VMEM_LIMIT_BYTES,
    )


def _rms(x, g):
    return x * lax.rsqrt(jnp.mean(x * x, axis=-1, keepdims=True) + EPS) * g


def _silu(x):
    return x * jax.nn.sigmoid(x)


def _dot(a, b):
    return jnp.dot(a, b, preferred_element_type=F32)


def _full(shape):
    nd = len(shape)
    return pl.BlockSpec(shape, lambda *_: (0,) * nd)


def _layer(w, i):
    return pl.BlockSpec((None,) + w.shape[1:], lambda *_: (i, 0, 0), pipeline_mode=pl.Buffered(1))


def _windows(w, i, col0, width):
    n = w.shape[2]
    assert width % WEIGHT_BLOCK == 0 and col0 % LANES == 0
    specs, offsets = [], []
    for c in range(col0, col0 + width, WEIGHT_BLOCK):
        start = min(c, n - WINDOW)
        specs.append(pl.BlockSpec((pl.Element(1), pl.Element(w.shape[1]), pl.Element(WINDOW)),
                                  lambda *_, start=start: (i, 0, start),
                                  pipeline_mode=pl.Buffered(1)))
        offsets.append(c - start)
    return specs, tuple(offsets)


def _proj(x, blocks, offsets):
    return jnp.concatenate(
        [_dot(x, w[0, :, o:o + WEIGHT_BLOCK]) for w, o in zip(blocks, offsets)], axis=-1)


def _swiglu_residual(x, g_ref, wg_ref, wu_ref, wd_ref):
    d_ff = wg_ref.shape[1]
    n = _rms(x, g_ref[...]).astype(BF16)
    acc = jnp.zeros(x.shape, F32)
    for c0 in range(0, d_ff, FFN_CHUNK):
        c1 = min(c0 + FFN_CHUNK, d_ff)
        act = (_silu(_dot(n, wg_ref[:, c0:c1])) * _dot(n, wu_ref[:, c0:c1])).astype(BF16)
        acc = acc + _dot(act, wd_ref[c0:c1, :])
    return x + 0.5 * acc


def _ffn_body(h_ref, g_ref, wg_ref, wu_ref, wd_ref, o_ref):
    o_ref[...] = _swiglu_residual(h_ref[...], g_ref, wg_ref, wu_ref, wd_ref)


def _ffn_ple_body(h_ref, g_ref, wg_ref, wu_ref, wd_ref, p_ref, gh_ref, wpg_ref, wpe_ref, gp_ref,
                  gf_ref, o_ref, *, final):
    x = _swiglu_residual(h_ref[...], g_ref, wg_ref, wu_ref, wd_ref)
    e = _rms(_dot(p_ref[...].astype(BF16), wpe_ref[...]), gp_ref[...])
    gate = jax.nn.sigmoid(_dot(_rms(x, gh_ref[...]).astype(BF16), wpg_ref[...]))
    out = x + gate * e
    if final:
        out = _rms(out, gf_ref[...])
    o_ref[...] = out


def _ffn(h, g, wg, wu, wd, layer, *, tm):
    n_tok, d = h.shape
    tile = pl.BlockSpec((tm, d), lambda i: (i, 0))
    return pl.pallas_call(
        _ffn_body,
        grid=(n_tok // tm,),
        in_specs=[tile, _full((1, d)), _layer(wg, layer), _layer(wu, layer), _layer(wd, layer)],
        out_specs=tile,
        out_shape=jax.ShapeDtypeStruct((n_tok, d), F32),
        compiler_params=_params(1),
        name="ffn",
    )(h, g, wg, wu, wd)


def _ffn_ple(h, g, wg, wu, wd, p, gh, wpg, wpe, gp, gf, layer, *, tm, final):
    n_tok, d = h.shape
    pd = p.shape[2]
    tile = pl.BlockSpec((tm, d), lambda i: (i, 0))
    return pl.pallas_call(
        functools.partial(_ffn_ple_body, final=final),
        grid=(n_tok // tm,),
        in_specs=[tile, _full((1, d)), _layer(wg, layer), _layer(wu, layer), _layer(wd, layer),
                  pl.BlockSpec((None, tm, pd), lambda i: (layer, i, 0)), _full((1, d)),
                  _layer(wpg, layer), _layer(wpe, layer), _full((1, d)), _full((1, d))],
        out_specs=tile,
        out_shape=jax.ShapeDtypeStruct((n_tok, d), F32),
        compiler_params=_params(1),
        name="ffn_ple",
    )(h, g, wg, wu, wd, p, gh, wpg, wpe, gp, gf)


def _conv_branch(u, wa, cw_ref, cb_ref, lng_ref, lnb_ref, glu_ref, ext_ref, y_ref, *, tm, conv_k):
    a = _proj(u, *wa)
    ch = a.shape[1] // 2
    sub = ch // LANES
    halo = CONV_HALO * sub
    glu = a[:, :ch] * jax.nn.sigmoid(a[:, ch:])
    for k in range(sub):
        glu_ref[:, k * SUBLANES:(k + 1) * SUBLANES, :] = glu[:, k * LANES:(k + 1) * LANES].reshape(
            tm // SUBLANES, SUBLANES, LANES)
    for t in range(tm):
        r, s = divmod(t, SUBLANES)
        ext_ref[halo + t * sub:halo + (t + 1) * sub, :] = glu_ref[r, pl.ds(s, sub, stride=SUBLANES), :]

    off = CONV_HALO - (conv_k - 1)
    taps = [cw_ref[j * sub:(j + 1) * sub, :] for j in range(conv_k)]
    bias = cb_ref[...]
    for t0 in range(0, tm, CONV_STEPS):
        acc = jnp.broadcast_to(bias, (CONV_STEPS, sub, LANES))
        for j in range(conv_k):
            r0 = (t0 + off + j) * sub
            acc = acc + taps[j] * ext_ref[r0:r0 + CONV_STEPS * sub, :].reshape(CONV_STEPS, sub, LANES)
        y_ref[t0 * sub:(t0 + CONV_STEPS) * sub, :] = acc.reshape(CONV_STEPS * sub, LANES)

    y = jnp.concatenate([y_ref[pl.ds(k, tm, stride=sub), :] for k in range(sub)], axis=-1)
    mu = jnp.mean(y, axis=-1, keepdims=True)
    yc = y - mu
    var = jnp.mean(yc * yc, axis=-1, keepdims=True)
    yn = yc * lax.rsqrt(var + EPS) * lng_ref[...] + lnb_ref[...]
    return _silu(yn).astype(BF16)


def _rotate(t, cos, sin):
    half = t.shape[-1] // 2
    t1, t2 = t[:, :half], t[:, half:]
    return jnp.concatenate([t1 * cos - t2 * sin, t2 * cos + t1 * sin], axis=-1)


def _log_gamma(hd):
    return math.log1p(-(2.0 ** (-5.0 - hd)))


def _scale_rows(t, col):
    return jnp.concatenate(
        [t[:, i:i + LANES] * col for i in range(0, t.shape[1], LANES)], axis=-1)


def _decay_tables(intra_ref, qdec_ref, kdec_ref):
    c = intra_ref.shape[1]
    row = lax.broadcasted_iota(jnp.int32, (c, c), 0)
    col = lax.broadcasted_iota(jnp.int32, (c, c), 1)
    rel = (row - col).astype(F32)
    idx = lax.broadcasted_iota(jnp.int32, (c, LANES), 0).astype(F32)
    for hd in range(RET_HEADS):
        lg = _log_gamma(hd)
        intra_ref[hd] = jnp.where(rel >= 0, jnp.exp(lg * jnp.maximum(rel, 0.0)), 0.0)
        qdec_ref[hd] = jnp.exp(lg * (idx + 1.0))
        kdec_ref[hd] = jnp.exp(lg * (c - 1.0 - idx))


def _ret_branch(u, cos, sin, wq, wk, wv, wg, intra_ref, qdec_ref, kdec_ref, state_ref, gated_ref):
    heads = range(RET_HEADS)
    c = u.shape[0]
    q = _proj(u, *wq)
    k = _proj(u, *wk)
    v = _proj(u, *wv).astype(BF16)
    gr = _proj(u, *wg)
    dk = q.shape[1] // RET_HEADS
    dv = v.shape[1] // RET_HEADS

    qb = [_rotate(q[:, hd * dk:(hd + 1) * dk], cos, sin) for hd in heads]
    kb = [_rotate(k[:, hd * dk:(hd + 1) * dk], cos, sin) * (dk ** -0.5) for hd in heads]
    vb = [v[:, hd * dv:(hd + 1) * dv] for hd in heads]
    state = [state_ref[hd] for hd in heads]

    scores = [lax.dot_general(qb[hd].astype(BF16), kb[hd].astype(BF16), (((1,), (1,)), ((), ())),
                              preferred_element_type=F32) for hd in heads]
    cross = [_dot(_scale_rows(qb[hd], qdec_ref[hd]).astype(BF16), state[hd].astype(BF16))
             for hd in heads]
    for hd in heads:
        state_ref[hd] = math.exp(_log_gamma(hd) * c) * state[hd] + lax.dot_general(
            _scale_rows(kb[hd], kdec_ref[hd]).astype(BF16), vb[hd], (((0,), (0,)), ((), ())),
            preferred_element_type=F32)
    inner = [_dot((scores[hd] * intra_ref[hd]).astype(BF16), vb[hd]) for hd in heads]
    for hd in heads:
        o = inner[hd] + cross[hd]
        o = o * lax.rsqrt(jnp.mean(o * o, axis=-1, keepdims=True) + EPS)
        gated_ref[:, hd * dv:(hd + 1) * dv] = (_silu(gr[:, hd * dv:(hd + 1) * dv]) * o).astype(BF16)
    return gated_ref[...]


def _mixer_body(h_ref, g_ref, cos_ref, sin_ref, *refs, tm, conv_k, groups):
    blocks, n_in = [], 0
    for count, offsets in groups:
        blocks.append((refs[n_in:n_in + count], offsets))
        n_in += count
    wa, wq, wk, wv, wg, wgt = blocks
    (bg_ref, cw_ref, cb_ref, lng_ref, lnb_ref, wco_ref, wro_ref, wout_ref, o_ref, glu_ref, ext_ref,
     y_ref, state_ref, gated_ref, intra_ref, qdec_ref, kdec_ref) = refs[n_in:]
    d = o_ref.shape[-1]
    halo = CONV_HALO * (wco_ref.shape[0] // LANES)
    s = pl.program_id(1)

    @pl.when(s == 0)
    def _():
        ext_ref[0:halo, :] = jnp.zeros((halo, LANES), F32)
        state_ref[...] = jnp.zeros(state_ref.shape, F32)
        _decay_tables(intra_ref, qdec_ref, kdec_ref)

    @pl.when(s > 0)
    def _():
        ext_ref[0:halo, :] = ext_ref[pl.ds(ext_ref.shape[0] - halo, halo), :]

    x = h_ref[0]
    u = _rms(x, g_ref[...]).astype(BF16)
    conv_act = _conv_branch(u, wa, cw_ref, cb_ref, lng_ref, lnb_ref, glu_ref, ext_ref, y_ref,
                            tm=tm, conv_k=conv_k)
    zg = _proj(u, *wgt)
    ret_act = _ret_branch(u, cos_ref[0], sin_ref[0], wq, wk, wv, wg, intra_ref, qdec_ref,
                          kdec_ref, state_ref, gated_ref)
    yb = _dot(ret_act, wro_ref[...])
    ya = _dot(conv_act, wco_ref[...])
    m = (jax.nn.sigmoid(zg[:, :d] + bg_ref[0:1, :]) * ya
         + jax.nn.sigmoid(zg[:, d:] + bg_ref[1:2, :]) * yb)
    o_ref[0] = x + _dot(m.astype(BF16), wout_ref[...])


def _mixer(h3, g, cos, sin, w_in, bg, conv_w, conv_b, ln_g, ln_b, wco, wro, wout, layer, *, tm):
    b, s, d = h3.shape
    ch = wco.shape[1]
    conv_k = conv_w.shape[0]
    sub = ch // LANES
    qk = d
    vw = wro.shape[1]
    dk, dv = qk // RET_HEADS, vw // RET_HEADS
    half = cos.shape[-1]
    assert conv_k - 1 <= CONV_HALO and tm % CONV_STEPS == 0 and sub == SUBLANES
    assert w_in.shape[2] == 2 * ch + 2 * qk + 2 * vw + 2 * d and dk == 2 * half
    cw = conv_w.reshape(conv_k * sub, LANES)
    cb = conv_b.reshape(sub, LANES)
    in_specs, groups, col = [], [], 0
    for width in (2 * ch, qk, qk, vw, vw, 2 * d):
        specs, offsets = _windows(w_in, layer, col, width)
        in_specs += specs
        groups.append((len(specs), offsets))
        col += width
    tile = pl.BlockSpec((1, tm, d), lambda i, j: (i, j, 0))
    rope = pl.BlockSpec((1, tm, half), lambda i, j: (i, j, 0))
    return pl.pallas_call(
        functools.partial(_mixer_body, tm=tm, conv_k=conv_k, groups=tuple(groups)),
        grid=(b, s // tm),
        in_specs=[
            tile, _full((1, d)), rope, rope, *in_specs,
            _full(bg.shape), _full(cw.shape), _full(cb.shape), _full((1, ch)), _full((1, ch)),
            _layer(wco, layer), _layer(wro, layer), _layer(wout, layer),
        ],
        out_specs=tile,
        out_shape=jax.ShapeDtypeStruct((b, s, d), F32),
        scratch_shapes=[
            pltpu.VMEM((tm // SUBLANES, sub * SUBLANES, LANES), F32),
            pltpu.VMEM(((tm + CONV_HALO) * sub, LANES), F32),
            pltpu.VMEM((tm * sub, LANES), F32),
            pltpu.VMEM((RET_HEADS, dk, dv), F32),
            pltpu.VMEM((tm, vw), BF16),
            pltpu.VMEM((RET_HEADS, tm, tm), F32),
            pltpu.VMEM((RET_HEADS, tm, LANES), F32),
            pltpu.VMEM((RET_HEADS, tm, LANES), F32),
        ],
        compiler_params=_params(2),
        name="mixer",
    )(h3, g, cos, sin, *([w_in] * len(in_specs)), bg, cw, cb, ln_g, ln_b, wco, wro, wout)


def _rope_body(pos_ref, inv_ref, cos_ref, sin_ref):
    ang = pos_ref[...].astype(F32) * inv_ref[...]
    cos_ref[...] = jnp.cos(ang)
    sin_ref[...] = jnp.sin(ang)


def _rope_tables(pos, inv, *, tm):
    n_tok = pos.shape[0]
    half = inv.shape[1]
    out = pl.BlockSpec((tm, half), lambda i: (i, 0))
    return pl.pallas_call(
        _rope_body,
        grid=(n_tok // tm,),
        in_specs=[pl.BlockSpec((tm, 1), lambda i: (i, 0)), _full(inv.shape)],
        out_specs=[out, out],
        out_shape=[jax.ShapeDtypeStruct((n_tok, half), F32)] * 2,
        compiler_params=_params(1),
        name="rope_tables",
    )(pos, inv)


def kernel(x, p, positions, ffn1_norm, ffn1_w_gate, ffn1_w_up, ffn1_w_down, mix_norm, w_in,
           b_gates, conv_w, conv_b, conv_ln_g, conv_ln_b, w_conv_o, w_ret_o, w_out, ffn2_norm,
           ffn2_w_gate, ffn2_w_up, ffn2_w_down, ple_norm, w_ple_gate, w_ple_proj,
           ple_proj_norm, final_norm):
    b, s, d = x.shape
    depth = p.shape[0]
    n_tok = b * s

    half = d // RET_HEADS // 2
    inv = (ROPE_BASE ** (-jnp.arange(half, dtype=F32) / half)).reshape(1, half)
    cos, sin = _rope_tables(positions.reshape(n_tok, 1), inv, tm=1024)
    cos, sin = cos.reshape(b, s, half), sin.reshape(b, s, half)

    def row(v):
        return v.reshape(1, -1)

    w1g, w1u, w1d = ffn1_w_gate.astype(BF16), ffn1_w_up.astype(BF16), ffn1_w_down.astype(BF16)
    w2g, w2u, w2d = ffn2_w_gate.astype(BF16), ffn2_w_up.astype(BF16), ffn2_w_down.astype(BF16)
    w_in_b, wco, wro, wout = (w_in.astype(BF16), w_conv_o.astype(BF16), w_ret_o.astype(BF16),
                              w_out.astype(BF16))
    wpg, wpe = w_ple_gate.astype(BF16), w_ple_proj.astype(BF16)
    p3 = p.reshape(depth, n_tok, -1)

    h = x.reshape(n_tok, d)
    for i in range(depth):
        h = _ffn(h, row(ffn1_norm[i]), w1g, w1u, w1d, i, tm=1024)
        h = _mixer(h.reshape(b, s, d), row(mix_norm[i]), cos, sin, w_in_b, b_gates[i], conv_w[i],
                   conv_b[i], row(conv_ln_g[i]), row(conv_ln_b[i]), wco, wro, wout, i,
                   tm=256).reshape(n_tok, d)
        h = _ffn_ple(h, row(ffn2_norm[i]), w2g, w2u, w2d, p3, row(ple_norm[i]), wpg, wpe,
                     row(ple_proj_norm[i]), row(final_norm), i, tm=1024, final=(i == depth - 1))
    return h.reshape(b, s, d)
```

```python
import functools
import math

import jax
import jax.numpy as jnp
from jax import lax
from jax.experimental import pallas as pl
from jax.experimental.pallas import tpu as pltpu

F32 = jnp.float32
BF16 = jnp.bfloat16

EPS = 1e-6
ROPE_BASE = 10000.0
RET_HEADS = 4
CONV_HALO = 32
CONV_STEPS = 16
LANES = 128
SUBLANES = 8
FFN_CHUNK = 768
WEIGHT_BLOCK = 1024
WINDOW = WEIGHT_BLOCK + LANES

VMEM_LIMIT_BYTES = 56 * 1024 * 1024


def _params(n_axes):
    return pltpu.CompilerParams(
        dimension_semantics=("arbitrary",) * n_axes,
        vmem_limit_bytes=VMEM_LIMIT_BYTES,
    )


def _rms(x, g):
    return x * lax.rsqrt(jnp.mean(x * x, axis=-1, keepdims=True) + EPS) * g


def _silu(x):
    return x * jax.nn.sigmoid(x)


def _dot(a, b):
    return jnp.dot(a, b, preferred_element_type=F32)


def _full(shape):
    nd = len(shape)
    return pl.BlockSpec(shape, lambda *_: (0,) * nd)


def _layer(w, i):
    return pl.BlockSpec((None,) + w.shape[1:], lambda *_: (i, 0, 0), pipeline_mode=pl.Buffered(1))


def _windows(w, i, col0, width):
    n = w.shape[2]
    assert width % WEIGHT_BLOCK == 0 and col0 % LANES == 0
    specs, offsets = [], []
    for c in range(col0, col0 + width, WEIGHT_BLOCK):
        start = min(c, n - WINDOW)
        specs.append(pl.BlockSpec((pl.Element(1), pl.Element(w.shape[1]), pl.Element(WINDOW)),
                                  lambda *_, start=start: (i, 0, start),
                                  pipeline_mode=pl.Buffered(1)))
        offsets.append(c - start)
    return specs, tuple(offsets)


def _proj(x, blocks, offsets):
    return jnp.concatenate(
        [_dot(x, w[0, :, o:o + WEIGHT_BLOCK]) for w, o in zip(blocks, offsets)], axis=-1)


def _swiglu_residual(x, g_ref, wg_ref, wu_ref, wd_ref):
    d_ff = wg_ref.shape[1]
    n = _rms(x, g_ref[...]).astype(BF16)
    acc = jnp.zeros(x.shape, F32)
    for c0 in range(0, d_ff, FFN_CHUNK):
        c1 = min(c0 + FFN_CHUNK, d_ff)
        act = (_silu(_dot(n, wg_ref[:, c0:c1])) * _dot(n, wu_ref[:, c0:c1])).astype(BF16)
        acc = acc + _dot(act, wd_ref[c0:c1, :])
    return x + 0.5 * acc


def _ffn_body(h_ref, g_ref, wg_ref, wu_ref, wd_ref, o_ref):
    o_ref[...] = _swiglu_residual(h_ref[...], g_ref, wg_ref, wu_ref, wd_ref)


def _ffn_ple_body(h_ref, g_ref, wg_ref, wu_ref, wd_ref, p_ref, gh_ref, wpg_ref, wpe_ref, gp_ref,
                  gf_ref, o_ref, *, final):
    x = _swiglu_residual(h_ref[...], g_ref, wg_ref, wu_ref, wd_ref)
    e = _rms(_dot(p_ref[...].astype(BF16), wpe_ref[...]), gp_ref[...])
    gate = jax.nn.sigmoid(_dot(_rms(x, gh_ref[...]).astype(BF16), wpg_ref[...]))
    out = x + gate * e
    if final:
        out = _rms(out, gf_ref[...])
    o_ref[...] = out


def _ffn(h, g, wg, wu, wd, layer, *, tm):
    n_tok, d = h.shape
    tile = pl.BlockSpec((tm, d), lambda i: (i, 0))
    return pl.pallas_call(
        _ffn_body,
        grid=(n_tok // tm,),
        in_specs=[tile, _full((1, d)), _layer(wg, layer), _layer(wu, layer), _layer(wd, layer)],
        out_specs=tile,
        out_shape=jax.ShapeDtypeStruct((n_tok, d), F32),
        compiler_params=_params(1),
        name="ffn",
    )(h, g, wg, wu, wd)


def _ffn_ple(h, g, wg, wu, wd, p, gh, wpg, wpe, gp, gf, layer, *, tm, final):
    n_tok, d = h.shape
    pd = p.shape[2]
    tile = pl.BlockSpec((tm, d), lambda i: (i, 0))
    return pl.pallas_call(
        functools.partial(_ffn_ple_body, final=final),
        grid=(n_tok // tm,),
        in_specs=[tile, _full((1, d)), _layer(wg, layer), _layer(wu, layer), _layer(wd, layer),
                  pl.BlockSpec((None, tm, pd), lambda i: (layer, i, 0)), _full((1, d)),
                  _layer(wpg, layer), _layer(wpe, layer), _full((1, d)), _full((1, d))],
        out_specs=tile,
        out_shape=jax.ShapeDtypeStruct((n_tok, d), F32),
        compiler_params=_params(1),
        name="ffn_ple",
    )(h, g, wg, wu, wd, p, gh, wpg, wpe, gp, gf)


def _conv_branch(u, wa, cw_ref, cb_ref, lng_ref, lnb_ref, glu_ref, ext_ref, y_ref, *, tm, conv_k):
    a = _proj(u, *wa)
    ch = a.shape[1] // 2
    sub = ch // LANES
    halo = CONV_HALO * sub
    glu = a[:, :ch] * jax.nn.sigmoid(a[:, ch:])
    for k in range(sub):
        glu_ref[:, k * SUBLANES:(k + 1) * SUBLANES, :] = glu[:, k * LANES:(k + 1) * LANES].reshape(
            tm // SUBLANES, SUBLANES, LANES)
    for t in range(tm):
        r, s = divmod(t, SUBLANES)
        ext_ref[halo + t * sub:halo + (t + 1) * sub, :] = glu_ref[r, pl.ds(s, sub, stride=SUBLANES), :]

    off = CONV_HALO - (conv_k - 1)
    taps = [cw_ref[j * sub:(j + 1) * sub, :] for j in range(conv_k)]
    bias = cb_ref[...]
    for t0 in range(0, tm, CONV_STEPS):
        acc = jnp.broadcast_to(bias, (CONV_STEPS, sub, LANES))
        for j in range(conv_k):
            r0 = (t0 + off + j) * sub
            acc = acc + taps[j] * ext_ref[r0:r0 + CONV_STEPS * sub, :].reshape(CONV_STEPS, sub, LANES)
        y_ref[t0 * sub:(t0 + CONV_STEPS) * sub, :] = acc.reshape(CONV_STEPS * sub, LANES)

    y = jnp.concatenate([y_ref[pl.ds(k, tm, stride=sub), :] for k in range(sub)], axis=-1)
    mu = jnp.mean(y, axis=-1, keepdims=True)
    yc = y - mu
    var = jnp.mean(yc * yc, axis=-1, keepdims=True)
    yn = yc * lax.rsqrt(var + EPS) * lng_ref[...] + lnb_ref[...]
    return _silu(yn).astype(BF16)


def _rotate(t, cos, sin):
    half = t.shape[-1] // 2
    t1, t2 = t[:, :half], t[:, half:]
    return jnp.concatenate([t1 * cos - t2 * sin, t2 * cos + t1 * sin], axis=-1)


def _log_gamma(hd):
    return math.log1p(-(2.0 ** (-5.0 - hd)))


def _scale_rows(t, col):
    return jnp.concatenate(
        [t[:, i:i + LANES] * col for i in range(0, t.shape[1], LANES)], axis=-1)


def _decay_tables(intra_ref, qdec_ref, kdec_ref):
    c = intra_ref.shape[1]
    row = lax.broadcasted_iota(jnp.int32, (c, c), 0)
    col = lax.broadcasted_iota(jnp.int32, (c, c), 1)
    rel = (row - col).astype(F32)
    idx = lax.broadcasted_iota(jnp.int32, (c, LANES), 0).astype(F32)
    for hd in range(RET_HEADS):
        lg = _log_gamma(hd)
        intra_ref[hd] = jnp.where(rel >= 0, jnp.exp(lg * jnp.maximum(rel, 0.0)), 0.0)
        qdec_ref[hd] = jnp.exp(lg * (idx + 1.0))
        kdec_ref[hd] = jnp.exp(lg * (c - 1.0 - idx))


def _ret_branch(u, cos, sin, wq, wk, wv, wg, intra_ref, qdec_ref, kdec_ref, state_ref, gated_ref):
    heads = range(RET_HEADS)
    c = u.shape[0]
    q = _proj(u, *wq)
    k = _proj(u, *wk)
    v = _proj(u, *wv).astype(BF16)
    gr = _proj(u, *wg)
    dk = q.shape[1] // RET_HEADS
    dv = v.shape[1] // RET_HEADS

    qb = [_rotate(q[:, hd * dk:(hd + 1) * dk], cos, sin) for hd in heads]
    kb = [_rotate(k[:, hd * dk:(hd + 1) * dk], cos, sin) * (dk ** -0.5) for hd in heads]
    vb = [v[:, hd * dv:(hd + 1) * dv] for hd in heads]
    state = [state_ref[hd] for hd in heads]

    scores = [lax.dot_general(qb[hd].astype(BF16), kb[hd].astype(BF16), (((1,), (1,)), ((), ())),
                              preferred_element_type=F32) for hd in heads]
    cross = [_dot(_scale_rows(qb[hd], qdec_ref[hd]).astype(BF16), state[hd].astype(BF16))
             for hd in heads]
    for hd in heads:
        state_ref[hd] = math.exp(_log_gamma(hd) * c) * state[hd] + lax.dot_general(
            _scale_rows(kb[hd], kdec_ref[hd]).astype(BF16), vb[hd], (((0,), (0,)), ((), ())),
            preferred_element_type=F32)
    inner = [_dot((scores[hd] * intra_ref[hd]).astype(BF16), vb[hd]) for hd in heads]
    for hd in heads:
        o = inner[hd] + cross[hd]
        o = o * lax.rsqrt(jnp.mean(o * o, axis=-1, keepdims=True) + EPS)
        gated_ref[:, hd * dv:(hd + 1) * dv] = (_silu(gr[:, hd * dv:(hd + 1) * dv]) * o).astype(BF16)
    return gated_ref[...]


def _mixer_body(h_ref, g_ref, ra_ref, rb_ref, *refs, tm, conv_k, groups, make_rope):
    blocks, n_in = [], 0
    for count, offsets in groups:
        blocks.append((refs[n_in:n_in + count], offsets))
        n_in += count
    wa, wq, wk, wv, wg, wgt = blocks
    rest = list(refs[n_in:])
    (bg_ref, cw_ref, cb_ref, lng_ref, lnb_ref, wco_ref, wro_ref, wout_ref, o_ref) = rest[:9]
    rope_out = rest[9:11] if make_rope else []
    (glu_ref, ext_ref, y_ref, state_ref, gated_ref, intra_ref, qdec_ref,
     kdec_ref) = rest[9 + len(rope_out):]
    if make_rope:
        ang = ra_ref[0].astype(F32) * rb_ref[...]
        cos, sin = jnp.cos(ang), jnp.sin(ang)
        rope_out[0][0] = cos
        rope_out[1][0] = sin
    else:
        cos, sin = ra_ref[0], rb_ref[0]
    d = o_ref.shape[-1]
    halo = CONV_HALO * (wco_ref.shape[0] // LANES)
    s = pl.program_id(1)

    @pl.when(s == 0)
    def _():
        ext_ref[0:halo, :] = jnp.zeros((halo, LANES), F32)
        state_ref[...] = jnp.zeros(state_ref.shape, F32)
        _decay_tables(intra_ref, qdec_ref, kdec_ref)

    @pl.when(s > 0)
    def _():
        ext_ref[0:halo, :] = ext_ref[pl.ds(ext_ref.shape[0] - halo, halo), :]

    x = h_ref[0]
    u = _rms(x, g_ref[...]).astype(BF16)
    conv_act = _conv_branch(u, wa, cw_ref, cb_ref, lng_ref, lnb_ref, glu_ref, ext_ref, y_ref,
                            tm=tm, conv_k=conv_k)
    zg = _proj(u, *wgt)
    ret_act = _ret_branch(u, cos, sin, wq, wk, wv, wg, intra_ref, qdec_ref, kdec_ref, state_ref,
                          gated_ref)
    yb = _dot(ret_act, wro_ref[...])
    ya = _dot(conv_act, wco_ref[...])
    m = (jax.nn.sigmoid(zg[:, :d] + bg_ref[0:1, :]) * ya
         + jax.nn.sigmoid(zg[:, d:] + bg_ref[1:2, :]) * yb)
    o_ref[0] = x + _dot(m.astype(BF16), wout_ref[...])


def _mixer(h3, g, ra, rb, w_in, bg, conv_w, conv_b, ln_g, ln_b, wco, wro, wout, layer, *, tm,
           make_rope):
    b, s, d = h3.shape
    ch = wco.shape[1]
    conv_k = conv_w.shape[0]
    sub = ch // LANES
    qk = d
    vw = wro.shape[1]
    dk, dv = qk // RET_HEADS, vw // RET_HEADS
    half = rb.shape[-1]
    assert conv_k - 1 <= CONV_HALO and tm % CONV_STEPS == 0 and sub == SUBLANES
    assert w_in.shape[2] == 2 * ch + 2 * qk + 2 * vw + 2 * d and dk == 2 * half
    cw = conv_w.reshape(conv_k * sub, LANES)
    cb = conv_b.reshape(sub, LANES)
    in_specs, groups, col = [], [], 0
    for width in (2 * ch, qk, qk, vw, vw, 2 * d):
        specs, offsets = _windows(w_in, layer, col, width)
        in_specs += specs
        groups.append((len(specs), offsets))
        col += width
    tile = pl.BlockSpec((1, tm, d), lambda i, j: (i, j, 0))
    rope = pl.BlockSpec((1, tm, half), lambda i, j: (i, j, 0))
    table = jax.ShapeDtypeStruct((b, s, half), F32)
    if make_rope:
        rope_in = [pl.BlockSpec((1, tm, 1), lambda i, j: (i, j, 0)), _full(rb.shape)]
    else:
        rope_in = [rope, rope]
    return pl.pallas_call(
        functools.partial(_mixer_body, tm=tm, conv_k=conv_k, groups=tuple(groups),
                          make_rope=make_rope),
        grid=(b, s // tm),
        in_specs=[
            tile, _full((1, d)), *rope_in, *in_specs,
            _full(bg.shape), _full(cw.shape), _full(cb.shape), _full((1, ch)), _full((1, ch)),
            _layer(wco, layer), _layer(wro, layer), _layer(wout, layer),
        ],
        out_specs=[tile, rope, rope] if make_rope else tile,
        out_shape=([jax.ShapeDtypeStruct((b, s, d), F32), table, table] if make_rope
                   else jax.ShapeDtypeStruct((b, s, d), F32)),
        scratch_shapes=[
            pltpu.VMEM((tm // SUBLANES, sub * SUBLANES, LANES), F32),
            pltpu.VMEM(((tm + CONV_HALO) * sub, LANES), F32),
            pltpu.VMEM((tm * sub, LANES), F32),
            pltpu.VMEM((RET_HEADS, dk, dv), F32),
            pltpu.VMEM((tm, vw), BF16),
            pltpu.VMEM((RET_HEADS, tm, tm), F32),
            pltpu.VMEM((RET_HEADS, tm, LANES), F32),
            pltpu.VMEM((RET_HEADS, tm, LANES), F32),
        ],
        compiler_params=_params(2),
        name="mixer",
    )(h3, g, ra, rb, *([w_in] * len(in_specs)), bg, cw, cb, ln_g, ln_b, wco, wro, wout)


def kernel(x, p, positions, ffn1_norm, ffn1_w_gate, ffn1_w_up, ffn1_w_down, mix_norm, w_in,
           b_gates, conv_w, conv_b, conv_ln_g, conv_ln_b, w_conv_o, w_ret_o, w_out, ffn2_norm,
           ffn2_w_gate, ffn2_w_up, ffn2_w_down, ple_norm, w_ple_gate, w_ple_proj,
           ple_proj_norm, final_norm):
    b, s, d = x.shape
    depth = p.shape[0]
    n_tok = b * s

    half = d // RET_HEADS // 2
    inv = (ROPE_BASE ** (-jnp.arange(half, dtype=F32) / half)).reshape(1, half)
    rope = [positions.reshape(b, s, 1), inv]

    def row(v):
        return v.reshape(1, -1)

    w1g, w1u, w1d = ffn1_w_gate.astype(BF16), ffn1_w_up.astype(BF16), ffn1_w_down.astype(BF16)
    w2g, w2u, w2d = ffn2_w_gate.astype(BF16), ffn2_w_up.astype(BF16), ffn2_w_down.astype(BF16)
    w_in_b, wco, wro, wout = (w_in.astype(BF16), w_conv_o.astype(BF16), w_ret_o.astype(BF16),
                              w_out.astype(BF16))
    wpg, wpe = w_ple_gate.astype(BF16), w_ple_proj.astype(BF16)
    p3 = p.reshape(depth, n_tok, -1)

    h = x.reshape(n_tok, d)
    for i in range(depth):
        h = _ffn(h, row(ffn1_norm[i]), w1g, w1u, w1d, i, tm=1024)
        res = _mixer(h.reshape(b, s, d), row(mix_norm[i]), *rope, w_in_b, b_gates[i], conv_w[i],
                     conv_b[i], row(conv_ln_g[i]), row(conv_ln_b[i]), wco, wro, wout, i,
                     tm=256, make_rope=(i == 0))
        if i == 0:
            res, *rope = res
        h = res.reshape(n_tok, d)
        h = _ffn_ple(h, row(ffn2_norm[i]), w2g, w2u, w2d, p3, row(ple_norm[i]), wpg, wpe,
                     row(ple_proj_norm[i]), row(final_norm), i, tm=1024, final=(i == depth - 1))
    return h.reshape(b, s, d)
```

```python
import functools
import math

import jax
import jax.numpy as jnp
from jax import lax
from jax.experimental import pallas as pl
from jax.experimental.pallas import tpu as pltpu

F32 = jnp.float32
BF16 = jnp.bfloat16

EPS = 1e-6
ROPE_BASE = 10000.0
RET_HEADS = 4
CONV_HALO = 32
CONV_STEPS = 16
LANES = 128
SUBLANES = 8
FFN_CHUNK = 768
WEIGHT_BLOCK = 1024
WINDOW = WEIGHT_BLOCK + LANES

VMEM_LIMIT_BYTES = 56 * 1024 * 1024


def _params(*semantics):
    return pltpu.CompilerParams(
        dimension_semantics=semantics,
        vmem_limit_bytes=VMEM_LIMIT_BYTES,
    )


def _rms(x, g):
    return x * lax.rsqrt(jnp.mean(x * x, axis=-1, keepdims=True) + EPS) * g


def _silu(x):
    return x * jax.nn.sigmoid(x)


def _dot(a, b):
    return jnp.dot(a, b, preferred_element_type=F32)


def _full(shape):
    nd = len(shape)
    return pl.BlockSpec(shape, lambda *_: (0,) * nd)


def _layer(w, i):
    return pl.BlockSpec((None,) + w.shape[1:], lambda *_: (i, 0, 0), pipeline_mode=pl.Buffered(1))


def _windows(w, i, col0, width):
    n = w.shape[2]
    assert width % WEIGHT_BLOCK == 0 and col0 % LANES == 0
    specs, offsets = [], []
    for c in range(col0, col0 + width, WEIGHT_BLOCK):
        start = min(c, n - WINDOW)
        specs.append(pl.BlockSpec((pl.Element(1), pl.Element(w.shape[1]), pl.Element(WINDOW)),
                                  lambda *_, start=start: (i, 0, start),
                                  pipeline_mode=pl.Buffered(1)))
        offsets.append(c - start)
    return specs, tuple(offsets)


def _proj(x, blocks, offsets):
    return jnp.concatenate(
        [_dot(x, w[0, :, o:o + WEIGHT_BLOCK]) for w, o in zip(blocks, offsets)], axis=-1)


def _swiglu_residual(x, g_ref, wg_ref, wu_ref, wd_ref):
    d_ff = wg_ref.shape[1]
    n = _rms(x, g_ref[...]).astype(BF16)
    acc = jnp.zeros(x.shape, F32)
    for c0 in range(0, d_ff, FFN_CHUNK):
        c1 = min(c0 + FFN_CHUNK, d_ff)
        act = (_silu(_dot(n, wg_ref[:, c0:c1])) * _dot(n, wu_ref[:, c0:c1])).astype(BF16)
        acc = acc + _dot(act, wd_ref[c0:c1, :])
    return x + 0.5 * acc


def _ffn_body(h_ref, g_ref, wg_ref, wu_ref, wd_ref, o_ref):
    o_ref[...] = _swiglu_residual(h_ref[...], g_ref, wg_ref, wu_ref, wd_ref)


def _ffn_ple_body(h_ref, g_ref, wg_ref, wu_ref, wd_ref, p_ref, gh_ref, wpg_ref, wpe_ref, gp_ref,
                  gf_ref, o_ref, *, final):
    x = _swiglu_residual(h_ref[...], g_ref, wg_ref, wu_ref, wd_ref)
    e = _rms(_dot(p_ref[...].astype(BF16), wpe_ref[...]), gp_ref[...])
    gate = jax.nn.sigmoid(_dot(_rms(x, gh_ref[...]).astype(BF16), wpg_ref[...]))
    out = x + gate * e
    if final:
        out = _rms(out, gf_ref[...])
    o_ref[...] = out


def _ffn(h, g, wg, wu, wd, layer, *, tm):
    n_tok, d = h.shape
    tile = pl.BlockSpec((tm, d), lambda i: (i, 0))
    return pl.pallas_call(
        _ffn_body,
        grid=(n_tok // tm,),
        in_specs=[tile, _full((1, d)), _layer(wg, layer), _layer(wu, layer), _layer(wd, layer)],
        out_specs=tile,
        out_shape=jax.ShapeDtypeStruct((n_tok, d), F32),
        compiler_params=_params("parallel"),
        name="ffn",
    )(h, g, wg, wu, wd)


def _ffn_ple(h, g, wg, wu, wd, p, gh, wpg, wpe, gp, gf, layer, *, tm, final):
    n_tok, d = h.shape
    pd = p.shape[2]
    tile = pl.BlockSpec((tm, d), lambda i: (i, 0))
    return pl.pallas_call(
        functools.partial(_ffn_ple_body, final=final),
        grid=(n_tok // tm,),
        in_specs=[tile, _full((1, d)), _layer(wg, layer), _layer(wu, layer), _layer(wd, layer),
                  pl.BlockSpec((None, tm, pd), lambda i: (layer, i, 0)), _full((1, d)),
                  _layer(wpg, layer), _layer(wpe, layer), _full((1, d)), _full((1, d))],
        out_specs=tile,
        out_shape=jax.ShapeDtypeStruct((n_tok, d), F32),
        compiler_params=_params("parallel"),
        name="ffn_ple",
    )(h, g, wg, wu, wd, p, gh, wpg, wpe, gp, gf)


def _conv_branch(u, wa, cw_ref, cb_ref, lng_ref, lnb_ref, glu_ref, ext_ref, y_ref, *, tm, conv_k):
    a = _proj(u, *wa)
    ch = a.shape[1] // 2
    sub = ch // LANES
    halo = CONV_HALO * sub
    glu = a[:, :ch] * jax.nn.sigmoid(a[:, ch:])
    for k in range(sub):
        glu_ref[:, k * SUBLANES:(k + 1) * SUBLANES, :] = glu[:, k * LANES:(k + 1) * LANES].reshape(
            tm // SUBLANES, SUBLANES, LANES)
    for t in range(tm):
        r, s = divmod(t, SUBLANES)
        ext_ref[halo + t * sub:halo + (t + 1) * sub, :] = glu_ref[r, pl.ds(s, sub, stride=SUBLANES), :]

    off = CONV_HALO - (conv_k - 1)
    taps = [cw_ref[j * sub:(j + 1) * sub, :] for j in range(conv_k)]
    bias = cb_ref[...]
    for t0 in range(0, tm, CONV_STEPS):
        acc = jnp.broadcast_to(bias, (CONV_STEPS, sub, LANES))
        for j in range(conv_k):
            r0 = (t0 + off + j) * sub
            acc = acc + taps[j] * ext_ref[r0:r0 + CONV_STEPS * sub, :].reshape(CONV_STEPS, sub, LANES)
        y_ref[t0 * sub:(t0 + CONV_STEPS) * sub, :] = acc.reshape(CONV_STEPS * sub, LANES)

    y = jnp.concatenate([y_ref[pl.ds(k, tm, stride=sub), :] for k in range(sub)], axis=-1)
    mu = jnp.mean(y, axis=-1, keepdims=True)
    yc = y - mu
    var = jnp.mean(yc * yc, axis=-1, keepdims=True)
    yn = yc * lax.rsqrt(var + EPS) * lng_ref[...] + lnb_ref[...]
    return _silu(yn).astype(BF16)


def _rotate(t, cos, sin):
    half = t.shape[-1] // 2
    t1, t2 = t[:, :half], t[:, half:]
    return jnp.concatenate([t1 * cos - t2 * sin, t2 * cos + t1 * sin], axis=-1)


def _log_gamma(hd):
    return math.log1p(-(2.0 ** (-5.0 - hd)))


def _scale_rows(t, col):
    return jnp.concatenate(
        [t[:, i:i + LANES] * col for i in range(0, t.shape[1], LANES)], axis=-1)


def _decay_tables(intra_ref, qdec_ref, kdec_ref):
    c = intra_ref.shape[1]
    row = lax.broadcasted_iota(jnp.int32, (c, c), 0)
    col = lax.broadcasted_iota(jnp.int32, (c, c), 1)
    rel = (row - col).astype(F32)
    idx = lax.broadcasted_iota(jnp.int32, (c, LANES), 0).astype(F32)
    for hd in range(RET_HEADS):
        lg = _log_gamma(hd)
        intra_ref[hd] = jnp.where(rel >= 0, jnp.exp(lg * jnp.maximum(rel, 0.0)), 0.0)
        qdec_ref[hd] = jnp.exp(lg * (idx + 1.0))
        kdec_ref[hd] = jnp.exp(lg * (c - 1.0 - idx))


def _ret_branch(u, cos, sin, wq, wk, wv, wg, intra_ref, qdec_ref, kdec_ref, state_ref, gated_ref):
    heads = range(RET_HEADS)
    c = u.shape[0]
    q = _proj(u, *wq)
    k = _proj(u, *wk)
    v = _proj(u, *wv).astype(BF16)
    gr = _proj(u, *wg)
    dk = q.shape[1] // RET_HEADS
    dv = v.shape[1] // RET_HEADS

    qb = [_rotate(q[:, hd * dk:(hd + 1) * dk], cos, sin) for hd in heads]
    kb = [_rotate(k[:, hd * dk:(hd + 1) * dk], cos, sin) * (dk ** -0.5) for hd in heads]
    vb = [v[:, hd * dv:(hd + 1) * dv] for hd in heads]
    state = [state_ref[hd] for hd in heads]

    scores = [lax.dot_general(qb[hd].astype(BF16), kb[hd].astype(BF16), (((1,), (1,)), ((), ())),
                              preferred_element_type=F32) for hd in heads]
    cross = [_dot(_scale_rows(qb[hd], qdec_ref[hd]).astype(BF16), state[hd].astype(BF16))
             for hd in heads]
    for hd in heads:
        state_ref[hd] = math.exp(_log_gamma(hd) * c) * state[hd] + lax.dot_general(
            _scale_rows(kb[hd], kdec_ref[hd]).astype(BF16), vb[hd], (((0,), (0,)), ((), ())),
            preferred_element_type=F32)
    inner = [_dot((scores[hd] * intra_ref[hd]).astype(BF16), vb[hd]) for hd in heads]
    for hd in heads:
        o = inner[hd] + cross[hd]
        o = o * lax.rsqrt(jnp.mean(o * o, axis=-1, keepdims=True) + EPS)
        gated_ref[:, hd * dv:(hd + 1) * dv] = (_silu(gr[:, hd * dv:(hd + 1) * dv]) * o).astype(BF16)
    return gated_ref[...]


def _mixer_body(h_ref, g_ref, ra_ref, rb_ref, *refs, tm, conv_k, groups, make_rope):
    blocks, n_in = [], 0
    for count, offsets in groups:
        blocks.append((refs[n_in:n_in + count], offsets))
        n_in += count
    wa, wq, wk, wv, wg, wgt = blocks
    rest = list(refs[n_in:])
    (bg_ref, cw_ref, cb_ref, lng_ref, lnb_ref, wco_ref, wro_ref, wout_ref, o_ref) = rest[:9]
    rope_out = rest[9:11] if make_rope else []
    (glu_ref, ext_ref, y_ref, state_ref, gated_ref, intra_ref, qdec_ref,
     kdec_ref) = rest[9 + len(rope_out):]
    if make_rope:
        ang = ra_ref[0].astype(F32) * rb_ref[...]
        cos, sin = jnp.cos(ang), jnp.sin(ang)
        rope_out[0][0] = cos
        rope_out[1][0] = sin
    else:
        cos, sin = ra_ref[0], rb_ref[0]
    d = o_ref.shape[-1]
    halo = CONV_HALO * (wco_ref.shape[0] // LANES)
    s = pl.program_id(1)

    @pl.when(s == 0)
    def _():
        ext_ref[0:halo, :] = jnp.zeros((halo, LANES), F32)
        state_ref[...] = jnp.zeros(state_ref.shape, F32)
        _decay_tables(intra_ref, qdec_ref, kdec_ref)

    @pl.when(s > 0)
    def _():
        ext_ref[0:halo, :] = ext_ref[pl.ds(ext_ref.shape[0] - halo, halo), :]

    x = h_ref[0]
    u = _rms(x, g_ref[...]).astype(BF16)
    conv_act = _conv_branch(u, wa, cw_ref, cb_ref, lng_ref, lnb_ref, glu_ref, ext_ref, y_ref,
                            tm=tm, conv_k=conv_k)
    zg = _proj(u, *wgt)
    ret_act = _ret_branch(u, cos, sin, wq, wk, wv, wg, intra_ref, qdec_ref, kdec_ref, state_ref,
                          gated_ref)
    yb = _dot(ret_act, wro_ref[...])
    ya = _dot(conv_act, wco_ref[...])
    m = (jax.nn.sigmoid(zg[:, :d] + bg_ref[0:1, :]) * ya
         + jax.nn.sigmoid(zg[:, d:] + bg_ref[1:2, :]) * yb)
    o_ref[0] = x + _dot(m.astype(BF16), wout_ref[...])


def _mixer(h3, g, ra, rb, w_in, bg, conv_w, conv_b, ln_g, ln_b, wco, wro, wout, layer, *, tm,
           make_rope):
    b, s, d = h3.shape
    ch = wco.shape[1]
    conv_k = conv_w.shape[0]
    sub = ch // LANES
    qk = d
    vw = wro.shape[1]
    dk, dv = qk // RET_HEADS, vw // RET_HEADS
    half = rb.shape[-1]
    assert conv_k - 1 <= CONV_HALO and tm % CONV_STEPS == 0 and sub == SUBLANES
    assert w_in.shape[2] == 2 * ch + 2 * qk + 2 * vw + 2 * d and dk == 2 * half
    cw = conv_w.reshape(conv_k * sub, LANES)
    cb = conv_b.reshape(sub, LANES)
    in_specs, groups, col = [], [], 0
    for width in (2 * ch, qk, qk, vw, vw, 2 * d):
        specs, offsets = _windows(w_in, layer, col, width)
        in_specs += specs
        groups.append((len(specs), offsets))
        col += width
    tile = pl.BlockSpec((1, tm, d), lambda i, j: (i, j, 0))
    rope = pl.BlockSpec((1, tm, half), lambda i, j: (i, j, 0))
    table = jax.ShapeDtypeStruct((b, s, half), F32)
    if make_rope:
        rope_in = [pl.BlockSpec((1, tm, 1), lambda i, j: (i, j, 0)), _full(rb.shape)]
    else:
        rope_in = [rope, rope]
    return pl.pallas_call(
        functools.partial(_mixer_body, tm=tm, conv_k=conv_k, groups=tuple(groups),
                          make_rope=make_rope),
        grid=(b, s // tm),
        in_specs=[
            tile, _full((1, d)), *rope_in, *in_specs,
            _full(bg.shape), _full(cw.shape), _full(cb.shape), _full((1, ch)), _full((1, ch)),
            _layer(wco, layer), _layer(wro, layer), _layer(wout, layer),
        ],
        out_specs=[tile, rope, rope] if make_rope else tile,
        out_shape=([jax.ShapeDtypeStruct((b, s, d), F32), table, table] if make_rope
                   else jax.ShapeDtypeStruct((b, s, d), F32)),
        scratch_shapes=[
            pltpu.VMEM((tm // SUBLANES, sub * SUBLANES, LANES), F32),
            pltpu.VMEM(((tm + CONV_HALO) * sub, LANES), F32),
            pltpu.VMEM((tm * sub, LANES), F32),
            pltpu.VMEM((RET_HEADS, dk, dv), F32),
            pltpu.VMEM((tm, vw), BF16),
            pltpu.VMEM((RET_HEADS, tm, tm), F32),
            pltpu.VMEM((RET_HEADS, tm, LANES), F32),
            pltpu.VMEM((RET_HEADS, tm, LANES), F32),
        ],
        compiler_params=_params("parallel", "arbitrary"),
        name="mixer",
    )(h3, g, ra, rb, *([w_in] * len(in_specs)), bg, cw, cb, ln_g, ln_b, wco, wro, wout)


def kernel(x, p, positions, ffn1_norm, ffn1_w_gate, ffn1_w_up, ffn1_w_down, mix_norm, w_in,
           b_gates, conv_w, conv_b, conv_ln_g, conv_ln_b, w_conv_o, w_ret_o, w_out, ffn2_norm,
           ffn2_w_gate, ffn2_w_up, ffn2_w_down, ple_norm, w_ple_gate, w_ple_proj,
           ple_proj_norm, final_norm):
    b, s, d = x.shape
    depth = p.shape[0]
    n_tok = b * s

    half = d // RET_HEADS // 2
    inv = (ROPE_BASE ** (-jnp.arange(half, dtype=F32) / half)).reshape(1, half)
    rope = [positions.reshape(b, s, 1), inv]

    def row(v):
        return v.reshape(1, -1)

    w1g, w1u, w1d = ffn1_w_gate.astype(BF16), ffn1_w_up.astype(BF16), ffn1_w_down.astype(BF16)
    w2g, w2u, w2d = ffn2_w_gate.astype(BF16), ffn2_w_up.astype(BF16), ffn2_w_down.astype(BF16)
    w_in_b, wco, wro, wout = (w_in.astype(BF16), w_conv_o.astype(BF16), w_ret_o.astype(BF16),
                              w_out.astype(BF16))
    wpg, wpe = w_ple_gate.astype(BF16), w_ple_proj.astype(BF16)
    p3 = p.reshape(depth, n_tok, -1)

    h = x.reshape(n_tok, d)
    for i in range(depth):
        h = _ffn(h, row(ffn1_norm[i]), w1g, w1u, w1d, i, tm=1024)
        res = _mixer(h.reshape(b, s, d), row(mix_norm[i]), *rope, w_in_b, b_gates[i], conv_w[i],
                     conv_b[i], row(conv_ln_g[i]), row(conv_ln_b[i]), wco, wro, wout, i,
                     tm=256, make_rope=(i == 0))
        if i == 0:
            res, *rope = res
        h = res.reshape(n_tok, d)
        h = _ffn_ple(h, row(ffn2_norm[i]), w2g, w2u, w2d, p3, row(ple_norm[i]), wpg, wpe,
                     row(ple_proj_norm[i]), row(final_norm), i, tm=1024, final=(i == depth - 1))
    return h.reshape(b, s, d)
```

```python
import functools
import math

import jax
import jax.numpy as jnp
from jax import lax
from jax.experimental import pallas as pl
from jax.experimental.pallas import tpu as pltpu

F32 = jnp.float32
BF16 = jnp.bfloat16

EPS = 1e-6
ROPE_BASE = 10000.0
RET_HEADS = 4
CONV_HALO = 32
CONV_STEPS = 16
LANES = 128
SUBLANES = 8
FFN_CHUNK = 768
WEIGHT_BLOCK = 1024
WINDOW = WEIGHT_BLOCK + LANES

VMEM_LIMIT_BYTES = 56 * 1024 * 1024


def _params(*semantics):
    return pltpu.CompilerParams(
        dimension_semantics=semantics,
        vmem_limit_bytes=VMEM_LIMIT_BYTES,
    )


def _rms(x, g):
    return x * lax.rsqrt(jnp.mean(x * x, axis=-1, keepdims=True) + EPS) * g


def _silu(x):
    return x * jax.nn.sigmoid(x)


def _dot(a, b):
    return jnp.dot(a, b, preferred_element_type=F32)


def _full(shape):
    nd = len(shape)
    return pl.BlockSpec(shape, lambda *_: (0,) * nd)


def _layer(w, i):
    return pl.BlockSpec((None,) + w.shape[1:], lambda *_: (i, 0, 0), pipeline_mode=pl.Buffered(1))


def _windows(w, i, col0, width):
    n = w.shape[2]
    assert width % WEIGHT_BLOCK == 0 and col0 % LANES == 0
    specs, offsets = [], []
    for c in range(col0, col0 + width, WEIGHT_BLOCK):
        start = min(c, n - WINDOW)
        specs.append(pl.BlockSpec((pl.Element(1), pl.Element(w.shape[1]), pl.Element(WINDOW)),
                                  lambda *_, start=start: (i, 0, start),
                                  pipeline_mode=pl.Buffered(1)))
        offsets.append(c - start)
    return specs, tuple(offsets)


def _proj(x, blocks, offsets):
    return jnp.concatenate(
        [_dot(x, w[0, :, o:o + WEIGHT_BLOCK]) for w, o in zip(blocks, offsets)], axis=-1)


def _swiglu_residual(x, g_ref, wg_ref, wu_ref, wd_ref):
    d_ff = wg_ref.shape[1]
    n = _rms(x, g_ref[...]).astype(BF16)
    acc = None
    for c0 in range(0, d_ff, FFN_CHUNK):
        c1 = min(c0 + FFN_CHUNK, d_ff)
        act = (_silu(_dot(n, wg_ref[:, c0:c1])) * _dot(n, wu_ref[:, c0:c1])).astype(BF16)
        part = _dot(act, wd_ref[c0:c1, :])
        acc = part if acc is None else acc + part
    return x + 0.5 * acc


def _ffn_body(h_ref, g_ref, wg_ref, wu_ref, wd_ref, o_ref):
    o_ref[...] = _swiglu_residual(h_ref[...], g_ref, wg_ref, wu_ref, wd_ref)


def _ffn_ple_body(h_ref, g_ref, wg_ref, wu_ref, wd_ref, p_ref, gh_ref, wpg_ref, wpe_ref, gp_ref,
                  gf_ref, o_ref, *, final):
    x = _swiglu_residual(h_ref[...], g_ref, wg_ref, wu_ref, wd_ref)
    e = _rms(_dot(p_ref[...].astype(BF16), wpe_ref[...]), gp_ref[...])
    gate = jax.nn.sigmoid(_dot(_rms(x, gh_ref[...]).astype(BF16), wpg_ref[...]))
    out = x + gate * e
    if final:
        out = _rms(out, gf_ref[...])
    o_ref[...] = out


def _ffn(h, g, wg, wu, wd, layer, *, tm):
    n_tok, d = h.shape
    tile = pl.BlockSpec((tm, d), lambda i: (i, 0))
    return pl.pallas_call(
        _ffn_body,
        grid=(n_tok // tm,),
        in_specs=[tile, _full((1, d)), _layer(wg, layer), _layer(wu, layer), _layer(wd, layer)],
        out_specs=tile,
        out_shape=jax.ShapeDtypeStruct((n_tok, d), F32),
        compiler_params=_params("parallel"),
        name="ffn",
    )(h, g, wg, wu, wd)


def _ffn_ple(h, g, wg, wu, wd, p, gh, wpg, wpe, gp, gf, layer, *, tm, final):
    n_tok, d = h.shape
    pd = p.shape[2]
    tile = pl.BlockSpec((tm, d), lambda i: (i, 0))
    return pl.pallas_call(
        functools.partial(_ffn_ple_body, final=final),
        grid=(n_tok // tm,),
        in_specs=[tile, _full((1, d)), _layer(wg, layer), _layer(wu, layer), _layer(wd, layer),
                  pl.BlockSpec((None, tm, pd), lambda i: (layer, i, 0)), _full((1, d)),
                  _layer(wpg, layer), _layer(wpe, layer), _full((1, d)), _full((1, d))],
        out_specs=tile,
        out_shape=jax.ShapeDtypeStruct((n_tok, d), F32),
        compiler_params=_params("parallel"),
        name="ffn_ple",
    )(h, g, wg, wu, wd, p, gh, wpg, wpe, gp, gf)


def _conv_branch(u, wa, cw_ref, cb_ref, lng_ref, lnb_ref, glu_ref, ext_ref, y_ref, *, tm, conv_k):
    a = _proj(u, *wa)
    ch = a.shape[1] // 2
    sub = ch // LANES
    halo = CONV_HALO * sub
    glu = a[:, :ch] * jax.nn.sigmoid(a[:, ch:])
    for k in range(sub):
        glu_ref[:, k * SUBLANES:(k + 1) * SUBLANES, :] = glu[:, k * LANES:(k + 1) * LANES].reshape(
            tm // SUBLANES, SUBLANES, LANES)
    for t in range(tm):
        r, s = divmod(t, SUBLANES)
        ext_ref[halo + t * sub:halo + (t + 1) * sub, :] = glu_ref[r, pl.ds(s, sub, stride=SUBLANES), :]

    off = CONV_HALO - (conv_k - 1)
    taps = [cw_ref[j * sub:(j + 1) * sub, :] for j in range(conv_k)]
    bias = cb_ref[...]
    for t0 in range(0, tm, CONV_STEPS):
        acc = jnp.broadcast_to(bias, (CONV_STEPS, sub, LANES))
        for j in range(conv_k):
            r0 = (t0 + off + j) * sub
            acc = acc + taps[j] * ext_ref[r0:r0 + CONV_STEPS * sub, :].reshape(CONV_STEPS, sub, LANES)
        y_ref[t0 * sub:(t0 + CONV_STEPS) * sub, :] = acc.reshape(CONV_STEPS * sub, LANES)

    y = jnp.concatenate([y_ref[pl.ds(k, tm, stride=sub), :] for k in range(sub)], axis=-1)
    mu = jnp.mean(y, axis=-1, keepdims=True)
    yc = y - mu
    var = jnp.mean(yc * yc, axis=-1, keepdims=True)
    yn = yc * lax.rsqrt(var + EPS) * lng_ref[...] + lnb_ref[...]
    return _silu(yn).astype(BF16)


def _rotate(t, cos, sin):
    half = t.shape[-1] // 2
    t1, t2 = t[:, :half], t[:, half:]
    return jnp.concatenate([t1 * cos - t2 * sin, t2 * cos + t1 * sin], axis=-1)


def _log_gamma(hd):
    return math.log1p(-(2.0 ** (-5.0 - hd)))


def _scale_rows(t, col):
    return jnp.concatenate(
        [t[:, i:i + LANES] * col for i in range(0, t.shape[1], LANES)], axis=-1)


def _decay_tables(intra_ref, qdec_ref, kdec_ref):
    c = intra_ref.shape[1]
    row = lax.broadcasted_iota(jnp.int32, (c, c), 0)
    col = lax.broadcasted_iota(jnp.int32, (c, c), 1)
    rel = (row - col).astype(F32)
    idx = lax.broadcasted_iota(jnp.int32, (c, LANES), 0).astype(F32)
    for hd in range(RET_HEADS):
        lg = _log_gamma(hd)
        intra_ref[hd] = jnp.where(rel >= 0, jnp.exp(lg * jnp.maximum(rel, 0.0)), 0.0)
        qdec_ref[hd] = jnp.exp(lg * (idx + 1.0))
        kdec_ref[hd] = jnp.exp(lg * (c - 1.0 - idx))


def _ret_branch(u, cos, sin, wq, wk, wv, wg, intra_ref, qdec_ref, kdec_ref, state_ref, gated_ref):
    heads = range(RET_HEADS)
    c = u.shape[0]
    q = _proj(u, *wq)
    k = _proj(u, *wk)
    v = _proj(u, *wv).astype(BF16)
    gr = _proj(u, *wg)
    dk = q.shape[1] // RET_HEADS
    dv = v.shape[1] // RET_HEADS

    qb = [_rotate(q[:, hd * dk:(hd + 1) * dk], cos, sin) for hd in heads]
    kb = [_rotate(k[:, hd * dk:(hd + 1) * dk], cos, sin) * (dk ** -0.5) for hd in heads]
    vb = [v[:, hd * dv:(hd + 1) * dv] for hd in heads]
    state = [state_ref[hd] for hd in heads]

    scores = [lax.dot_general(qb[hd].astype(BF16), kb[hd].astype(BF16), (((1,), (1,)), ((), ())),
                              preferred_element_type=F32) for hd in heads]
    cross = [_dot(_scale_rows(qb[hd], qdec_ref[hd]).astype(BF16), state[hd].astype(BF16))
             for hd in heads]
    for hd in heads:
        state_ref[hd] = math.exp(_log_gamma(hd) * c) * state[hd] + lax.dot_general(
            _scale_rows(kb[hd], kdec_ref[hd]).astype(BF16), vb[hd], (((0,), (0,)), ((), ())),
            preferred_element_type=F32)
    inner = [_dot((scores[hd] * intra_ref[hd]).astype(BF16), vb[hd]) for hd in heads]
    for hd in heads:
        o = inner[hd] + cross[hd]
        o = o * lax.rsqrt(jnp.mean(o * o, axis=-1, keepdims=True) + EPS)
        gated_ref[:, hd * dv:(hd + 1) * dv] = (_silu(gr[:, hd * dv:(hd + 1) * dv]) * o).astype(BF16)
    return gated_ref[...]


def _mixer_body(h_ref, g_ref, ra_ref, rb_ref, *refs, tm, conv_k, groups, make_rope):
    blocks, n_in = [], 0
    for count, offsets in groups:
        blocks.append((refs[n_in:n_in + count], offsets))
        n_in += count
    wa, wq, wk, wv, wg, wgt = blocks
    rest = list(refs[n_in:])
    (bg_ref, cw_ref, cb_ref, lng_ref, lnb_ref, wco_ref, wro_ref, wout_ref, o_ref) = rest[:9]
    rope_out = rest[9:11] if make_rope else []
    (glu_ref, ext_ref, y_ref, state_ref, gated_ref, intra_ref, qdec_ref,
     kdec_ref) = rest[9 + len(rope_out):]
    if make_rope:
        ang = ra_ref[0].astype(F32) * rb_ref[...]
        cos, sin = jnp.cos(ang), jnp.sin(ang)
        rope_out[0][0] = cos
        rope_out[1][0] = sin
    else:
        cos, sin = ra_ref[0], rb_ref[0]
    d = o_ref.shape[-1]
    halo = CONV_HALO * (wco_ref.shape[0] // LANES)
    s = pl.program_id(1)

    @pl.when(s == 0)
    def _():
        ext_ref[0:halo, :] = jnp.zeros((halo, LANES), F32)
        state_ref[...] = jnp.zeros(state_ref.shape, F32)
        _decay_tables(intra_ref, qdec_ref, kdec_ref)

    @pl.when(s > 0)
    def _():
        ext_ref[0:halo, :] = ext_ref[pl.ds(ext_ref.shape[0] - halo, halo), :]

    x = h_ref[0]
    u = _rms(x, g_ref[...]).astype(BF16)
    conv_act = _conv_branch(u, wa, cw_ref, cb_ref, lng_ref, lnb_ref, glu_ref, ext_ref, y_ref,
                            tm=tm, conv_k=conv_k)
    zg = _proj(u, *wgt)
    ret_act = _ret_branch(u, cos, sin, wq, wk, wv, wg, intra_ref, qdec_ref, kdec_ref, state_ref,
                          gated_ref)
    yb = _dot(ret_act, wro_ref[...])
    ya = _dot(conv_act, wco_ref[...])
    m = (jax.nn.sigmoid(zg[:, :d] + bg_ref[0:1, :]) * ya
         + jax.nn.sigmoid(zg[:, d:] + bg_ref[1:2, :]) * yb)
    o_ref[0] = x + _dot(m.astype(BF16), wout_ref[...])


def _mixer(h3, g, ra, rb, w_in, bg, conv_w, conv_b, ln_g, ln_b, wco, wro, wout, layer, *, tm,
           make_rope):
    b, s, d = h3.shape
    ch = wco.shape[1]
    conv_k = conv_w.shape[0]
    sub = ch // LANES
    qk = d
    vw = wro.shape[1]
    dk, dv = qk // RET_HEADS, vw // RET_HEADS
    half = rb.shape[-1]
    assert conv_k - 1 <= CONV_HALO and tm % CONV_STEPS == 0 and sub == SUBLANES
    assert w_in.shape[2] == 2 * ch + 2 * qk + 2 * vw + 2 * d and dk == 2 * half
    cw = conv_w.reshape(conv_k * sub, LANES)
    cb = conv_b.reshape(sub, LANES)
    in_specs, groups, col = [], [], 0
    for width in (2 * ch, qk, qk, vw, vw, 2 * d):
        specs, offsets = _windows(w_in, layer, col, width)
        in_specs += specs
        groups.append((len(specs), offsets))
        col += width
    tile = pl.BlockSpec((1, tm, d), lambda i, j: (i, j, 0))
    rope = pl.BlockSpec((1, tm, half), lambda i, j: (i, j, 0))
    table = jax.ShapeDtypeStruct((b, s, half), F32)
    if make_rope:
        rope_in = [pl.BlockSpec((1, tm, 1), lambda i, j: (i, j, 0)), _full(rb.shape)]
    else:
        rope_in = [rope, rope]
    return pl.pallas_call(
        functools.partial(_mixer_body, tm=tm, conv_k=conv_k, groups=tuple(groups),
                          make_rope=make_rope),
        grid=(b, s // tm),
        in_specs=[
            tile, _full((1, d)), *rope_in, *in_specs,
            _full(bg.shape), _full(cw.shape), _full(cb.shape), _full((1, ch)), _full((1, ch)),
            _layer(wco, layer), _layer(wro, layer), _layer(wout, layer),
        ],
        out_specs=[tile, rope, rope] if make_rope else tile,
        out_shape=([jax.ShapeDtypeStruct((b, s, d), F32), table, table] if make_rope
                   else jax.ShapeDtypeStruct((b, s, d), F32)),
        scratch_shapes=[
            pltpu.VMEM((tm // SUBLANES, sub * SUBLANES, LANES), F32),
            pltpu.VMEM(((tm + CONV_HALO) * sub, LANES), F32),
            pltpu.VMEM((tm * sub, LANES), F32),
            pltpu.VMEM((RET_HEADS, dk, dv), F32),
            pltpu.VMEM((tm, vw), BF16),
            pltpu.VMEM((RET_HEADS, tm, tm), F32),
            pltpu.VMEM((RET_HEADS, tm, LANES), F32),
            pltpu.VMEM((RET_HEADS, tm, LANES), F32),
        ],
        compiler_params=_params("parallel", "arbitrary"),
        name="mixer",
    )(h3, g, ra, rb, *([w_in] * len(in_specs)), bg, cw, cb, ln_g, ln_b, wco, wro, wout)


def kernel(x, p, positions, ffn1_norm, ffn1_w_gate, ffn1_w_up, ffn1_w_down, mix_norm, w_in,
           b_gates, conv_w, conv_b, conv_ln_g, conv_ln_b, w_conv_o, w_ret_o, w_out, ffn2_norm,
           ffn2_w_gate, ffn2_w_up, ffn2_w_down, ple_norm, w_ple_gate, w_ple_proj,
           ple_proj_norm, final_norm):
    b, s, d = x.shape
    depth = p.shape[0]
    n_tok = b * s

    half = d // RET_HEADS // 2
    inv = (ROPE_BASE ** (-jnp.arange(half, dtype=F32) / half)).reshape(1, half)
    rope = [positions.reshape(b, s, 1), inv]

    def row(v):
        return v.reshape(1, -1)

    w1g, w1u, w1d = ffn1_w_gate.astype(BF16), ffn1_w_up.astype(BF16), ffn1_w_down.astype(BF16)
    w2g, w2u, w2d = ffn2_w_gate.astype(BF16), ffn2_w_up.astype(BF16), ffn2_w_down.astype(BF16)
    w_in_b, wco, wro, wout = (w_in.astype(BF16), w_conv_o.astype(BF16), w_ret_o.astype(BF16),
                              w_out.astype(BF16))
    wpg, wpe = w_ple_gate.astype(BF16), w_ple_proj.astype(BF16)
    p3 = p.reshape(depth, n_tok, -1)

    h = x.reshape(n_tok, d)
    for i in range(depth):
        h = _ffn(h, row(ffn1_norm[i]), w1g, w1u, w1d, i, tm=1024)
        res = _mixer(h.reshape(b, s, d), row(mix_norm[i]), *rope, w_in_b, b_gates[i], conv_w[i],
                     conv_b[i], row(conv_ln_g[i]), row(conv_ln_b[i]), wco, wro, wout, i,
                     tm=256, make_rope=(i == 0))
        if i == 0:
            res, *rope = res
        h = res.reshape(n_tok, d)
        h = _ffn_ple(h, row(ffn2_norm[i]), w2g, w2u, w2d, p3, row(ple_norm[i]), wpg, wpe,
                     row(ple_proj_norm[i]), row(final_norm), i, tm=1024, final=(i == depth - 1))
    return h.reshape(b, s, d)
```
